```python
import math
import jax, jax.numpy as jnp
from jax import lax
import numpy as np

D_MODEL = 2048
BATCH = 8
SEQ = 4096
DEPTH = 4

D_MIX = D_MODEL
ML_HEADS = 4
ML_DH = D_MIX // 16
ML_W = ML_HEADS * ML_DH
ML_CHUNK = 64
ML_CONV = 4
GLA_HEADS = 4
GLA_DK = D_MIX // 32
GLA_DV = D_MIX // 16
GLA_KW = GLA_HEADS * GLA_DK
GLA_VW = GLA_HEADS * GLA_DV
GLA_RANK = 16
GLA_TAU = 16.0
GLA_CHUNK = 64
MLA_HEADS = 8
MLA_NOPE = 128
MLA_ROPE = 64
MLA_DV = 128
MLA_VW = MLA_HEADS * MLA_DV
MLA_Q_RANK = D_MODEL // 4
MLA_KV_RANK = D_MODEL // 4
ATTN_BLOCK = 128
ROPE_THETA = 10000.0
MAX_POS_OFFSET = 1024
FFN_DIM = 5632
FFN_RES_WEIGHT = 0.5
DEEPNORM_ALPHA = (2.0 * DEPTH) ** 0.25
DEEPNORM_BETA = (8.0 * DEPTH) ** -0.25
NORM_EPS = 1e-5
ADA_STD = 0.1

IN_SIZES = (ML_W, ML_W, ML_W, ML_HEADS, ML_HEADS, ML_W,
            GLA_KW, GLA_KW, GLA_VW, GLA_RANK, GLA_VW,
            MLA_Q_RANK, MLA_KV_RANK, MLA_ROPE)
D_IN = sum(IN_SIZES)
IN_SPLITS = tuple(int(s) for s in np.cumsum(IN_SIZES)[:-1])

kernel_name = "hybrid_mlstm_gla_mla_macaron_deepnorm"


def _layer_norm(x, g, b):
    xf = x.astype(jnp.float32)
    mu = jnp.mean(xf, -1, keepdims=True)
    var = jnp.mean(jnp.square(xf - mu), -1, keepdims=True)
    return ((xf - mu) * lax.rsqrt(var + NORM_EPS) * g + b).astype(x.dtype)


def _head_layer_norm(x):
    xf = x.astype(jnp.float32)
    mu = jnp.mean(xf, -1, keepdims=True)
    var = jnp.mean(jnp.square(xf - mu), -1, keepdims=True)
    return ((xf - mu) * lax.rsqrt(var + NORM_EPS)).astype(x.dtype)


def _rms_norm(x, g=None):
    xf = x.astype(jnp.float32)
    y = xf * lax.rsqrt(jnp.mean(jnp.square(xf), -1, keepdims=True) + NORM_EPS)
    if g is not None:
        y = y * g
    return y.astype(x.dtype)


def _modulate(x, m):
    return x * (1.0 + m[:, 1][:, None, :]) + m[:, 0][:, None, :]


def _gate(m, y):
    return (1.0 + m[:, 2][:, None, :]) * y


def _swiglu(u, w_i, w_o):
    g, up = jnp.split(u @ w_i, 2, axis=-1)
    return (jax.nn.silu(g) * up) @ w_o


def _split_heads(t, n):
    b, s, w = t.shape
    return t.reshape(b, s, n, w // n).transpose(0, 2, 1, 3)


def _merge_heads(t):
    b, h, s, d = t.shape
    return t.transpose(0, 2, 1, 3).reshape(b, s, h * d)


def _causal_dwconv(x, w):
    k = w.shape[0]
    return lax.conv_general_dilated(x, w[:, None, :].astype(x.dtype), window_strides=(1,),
                                    padding=[(k - 1, 0)],
                                    dimension_numbers=('NWC', 'WIO', 'NWC'),
                                    feature_group_count=x.shape[-1])


def _rope_tables(positions):
    half = MLA_ROPE // 2
    inv_freq = ROPE_THETA ** (-jnp.arange(half, dtype=jnp.float32) / half)
    ang = positions.astype(jnp.float32)[..., None] * inv_freq
    return jnp.cos(ang)[:, :, None, :], jnp.sin(ang)[:, :, None, :]


def _apply_rope(x, cos, sin):
    half = x.shape[-1] // 2
    x1 = x[..., :half].astype(jnp.float32)
    x2 = x[..., half:].astype(jnp.float32)
    return jnp.concatenate([x1 * cos - x2 * sin, x1 * sin + x2 * cos], -1).astype(x.dtype)


def _mlstm_chunkwise(q, k, v, i_pre, f_pre):
    bsz, nh, seq, dh = q.shape
    nc = seq // ML_CHUNK
    q, k, v = (t.reshape(bsz, nh, nc, ML_CHUNK, dh) for t in (q, k, v))
    log_i = i_pre.reshape(bsz, nh, nc, ML_CHUNK)
    b = jnp.cumsum(jax.nn.log_sigmoid(f_pre).reshape(bsz, nh, nc, ML_CHUNK), axis=-1)
    b_last = b[..., -1]
    a = b_last[..., None] - b + log_i
    a_max = jnp.max(a, -1)
    w = jnp.exp(a - a_max[..., None])
    c_loc = jnp.einsum('bhcs,bhcsd,bhcse->bhcde', w, k, v)
    n_loc = jnp.einsum('bhcs,bhcsd->bhcd', w, k)

    def step(carry, xs):
        c_st, n_st, m_st = carry
        bl, am, cl, nl = xs
        m_new = jnp.maximum(bl + m_st, am)
        dec = jnp.exp(bl + m_st - m_new)
        inj = jnp.exp(am - m_new)
        c_new = dec[..., None, None] * c_st + inj[..., None, None] * cl
        n_new = dec[..., None] * n_st + inj[..., None] * nl
        return (c_new, n_new, m_new), (c_st, n_st, m_st)

    init = (jnp.zeros((bsz, nh, dh, dh), jnp.float32),
            jnp.zeros((bsz, nh, dh), jnp.float32),
            jnp.zeros((bsz, nh), jnp.float32))
    xs = tuple(jnp.moveaxis(t, 2, 0) for t in (b_last, a_max, c_loc, n_loc))
    _, (c_prev, n_prev, m_prev) = lax.scan(step, init, xs)
    c_prev, n_prev, m_prev = (jnp.moveaxis(t, 0, 2) for t in (c_prev, n_prev, m_prev))

    causal = jnp.tril(jnp.ones((ML_CHUNK, ML_CHUNK), bool))
    d_mat = jnp.where(causal, b[..., :, None] - b[..., None, :] + log_i[..., None, :], -jnp.inf)
    m_inter = b + m_prev[..., None]
    m_t = jnp.maximum(m_inter, jnp.max(d_mat, -1))
    s = jnp.einsum('bhctd,bhcsd->bhcts', q, k) * jnp.exp(d_mat - m_t[..., None])
    inter = jnp.exp(m_inter - m_t)
    num = (jnp.einsum('bhcts,bhcse->bhcte', s, v)
           + inter[..., None] * jnp.einsum('bhctd,bhcde->bhcte', q, c_prev))
    den = jnp.sum(s, -1) + inter * jnp.einsum('bhctd,bhcd->bhct', q, n_prev)
    h = num / jnp.maximum(jnp.abs(den), jnp.exp(-m_t))[..., None]
    return h.reshape(bsz, nh, seq, dh).astype(v.dtype)


def _gla_chunkwise(q, k, v, log_a):
    bsz, nh, seq, dk = q.shape
    dv = v.shape[-1]
    nc = seq // GLA_CHUNK
    q = q.astype(jnp.float32).reshape(bsz, nh, nc, GLA_CHUNK, dk)
    k = k.astype(jnp.float32).reshape(bsz, nh, nc, GLA_CHUNK, dk)
    v = v.reshape(bsz, nh, nc, GLA_CHUNK, dv)
    bc = jnp.cumsum(log_a.reshape(bsz, nh, nc, GLA_CHUNK, dk), axis=-2)
    b_last = bc[..., -1, :]
    q_dec = q * jnp.exp(bc)
    causal = jnp.tril(jnp.ones((GLA_CHUNK, GLA_CHUNK), bool))
    attn = jnp.where(causal, jnp.einsum('bhctd,bhcsd->bhcts', q_dec, k * jnp.exp(-bc)), 0.0)
    o_intra = jnp.einsum('bhcts,bhcse->bhcte', attn, v)
    s_loc = jnp.einsum('bhcsd,bhcse->bhcde', k * jnp.exp(b_last[..., None, :] - bc), v)

    def step(s_st, xs):
        bl, sl = xs
        return jnp.exp(bl)[..., None] * s_st + sl, s_st

    _, s_prev = lax.scan(step, jnp.zeros((bsz, nh, dk, dv), jnp.float32),
                         (jnp.moveaxis(b_last, 2, 0), jnp.moveaxis(s_loc, 2, 0)))
    o_inter = jnp.einsum('bhctd,bhcde->bhcte', q_dec, jnp.moveaxis(s_prev, 0, 2))
    return (o_intra + o_inter).reshape(bsz, nh, seq, dv).astype(v.dtype)


def _mla_causal_attention(q_nope, q_rope, k_nope, k_rope, v):
    bsz, seq, nh, _ = q_nope.shape
    nb = seq // ATTN_BLOCK
    scale = (MLA_NOPE + MLA_ROPE) ** -0.5
    qn = q_nope.reshape(bsz, nb, ATTN_BLOCK, nh, MLA_NOPE).transpose(1, 0, 2, 3, 4)
    qr = q_rope.reshape(bsz, nb, ATTN_BLOCK, nh, MLA_ROPE).transpose(1, 0, 2, 3, 4)
    key_pos = jnp.arange(seq)

    def block(args):
        qn_b, qr_b, blk = args
        s = (jnp.einsum('bqhd,bkhd->bhqk', qn_b, k_nope)
             + jnp.einsum('bqhr,bkr->bhqk', qr_b, k_rope)).astype(jnp.float32) * scale
        q_pos = blk * ATTN_BLOCK + jnp.arange(ATTN_BLOCK)
        s = jnp.where(key_pos[None, :] <= q_pos[:, None], s, -jnp.inf)
        p = jax.nn.softmax(s, axis=-1).astype(v.dtype)
        return jnp.einsum('bhqk,bkhe->bqhe', p, v)

    out = lax.map(block, (qn, qr, jnp.arange(nb)))
    return out.transpose(1, 0, 2, 3, 4).reshape(bsz, seq, nh * MLA_DV)


def _token_mixer(u, cos, sin, w_in, ml_conv, ml_bi, ml_bf, gla_wg, gla_bg,
                 mla_gq, mla_wuq, mla_gkv, mla_wuk, mla_wuv, w_out):
    bsz, seq, _ = u.shape
    (ml_q, ml_k, ml_v, ml_i, ml_f, ml_o, gl_q, gl_k, gl_v, gl_lr, gl_r,
     c_q, c_kv, k_r) = jnp.split(u @ w_in, IN_SPLITS, axis=-1)

    qk = jax.nn.silu(_causal_dwconv(jnp.concatenate([ml_q, ml_k], -1), ml_conv))
    ml_q, ml_k = jnp.split(qk, 2, axis=-1)
    i_pre = jnp.swapaxes(ml_i.astype(jnp.float32) + ml_bi, 1, 2)
    f_pre = jnp.swapaxes(ml_f.astype(jnp.float32) + ml_bf, 1, 2)
    h = _mlstm_chunkwise(_split_heads(ml_q, ML_HEADS),
                         _split_heads(ml_k, ML_HEADS) * ML_DH ** -0.5,
                         _split_heads(ml_v, ML_HEADS), i_pre, f_pre)
    y_ml = jax.nn.sigmoid(ml_o) * _merge_heads(_head_layer_norm(h))

    log_a = jax.nn.log_sigmoid((gl_lr @ gla_wg + gla_bg).astype(jnp.float32)) / GLA_TAU
    o = _gla_chunkwise(_split_heads(gl_q, GLA_HEADS) * GLA_DK ** -0.5,
                       _split_heads(gl_k, GLA_HEADS), _split_heads(gl_v, GLA_HEADS),
                       _split_heads(log_a, GLA_HEADS))
    y_gla = jax.nn.silu(gl_r) * _merge_heads(_rms_norm(o))

    q = (_rms_norm(c_q, mla_gq) @ mla_wuq).reshape(bsz, seq, MLA_HEADS, MLA_NOPE + MLA_ROPE)
    q_nope = q[..., :MLA_NOPE]
    q_rope = _apply_rope(q[..., MLA_NOPE:], cos, sin)
    ckv = _rms_norm(c_kv, mla_gkv)
    k_nope = (ckv @ mla_wuk).reshape(bsz, seq, MLA_HEADS, MLA_NOPE)
    v = (ckv @ mla_wuv).reshape(bsz, seq, MLA_HEADS, MLA_DV)
    k_rope = _apply_rope(k_r[:, :, None, :], cos, sin)[:, :, 0]
    y_mla = _mla_causal_attention(q_nope, q_rope, k_nope, k_rope, v)

    return jnp.concatenate([y_ml, y_gla, y_mla], -1) @ w_out


def setup_inputs(seed: int = 0) -> dict:
    key = jax.random.key(seed)
    ks = jax.random.split(key, 24)

    def nrm(k, shape, std):
        return std * jax.random.normal(k, shape, jnp.float32)

    x = nrm(ks[0], (BATCH, SEQ, D_MODEL), 1.0)
    c = nrm(ks[1], (BATCH, D_MODEL), 1.0)
    positions = (jax.random.randint(ks[2], (BATCH, 1), 0, MAX_POS_OFFSET, jnp.int32)
                 + jnp.arange(SEQ, dtype=jnp.int32)[None, :])
    w_ada = nrm(ks[3], (DEPTH, D_MODEL, 9 * D_MODEL), ADA_STD * D_MODEL ** -0.5)
    b_ada = nrm(ks[4], (DEPTH, 9 * D_MODEL), 0.01)
    ln_g = 1.0 + nrm(ks[5], (DEPTH, 3, D_MODEL), 0.02)
    ln_b = nrm(ks[6], (DEPTH, 3, D_MODEL), 0.02)
    ffn1_wi = nrm(ks[7], (DEPTH, D_MODEL, 2 * FFN_DIM), D_MODEL ** -0.5)
    ffn1_wo = nrm(ks[8], (DEPTH, FFN_DIM, D_MODEL), DEEPNORM_BETA * FFN_DIM ** -0.5)
    ffn2_wi = nrm(ks[9], (DEPTH, D_MODEL, 2 * FFN_DIM), D_MODEL ** -0.5)
    ffn2_wo = nrm(ks[10], (DEPTH, FFN_DIM, D_MODEL), DEEPNORM_BETA * FFN_DIM ** -0.5)
    w_in = nrm(ks[11], (DEPTH, D_MODEL, D_IN), D_MODEL ** -0.5)
    ml_conv = nrm(ks[12], (DEPTH, ML_CONV, 2 * ML_W), ML_CONV ** -0.5)
    ml_bi = nrm(ks[13], (DEPTH, ML_HEADS), 0.1)
    ml_bf = (jnp.linspace(3.0, 6.0, ML_HEADS, dtype=jnp.float32)[None, :]
             + nrm(ks[14], (DEPTH, ML_HEADS), 0.1))
    gla_wg = nrm(ks[15], (DEPTH, GLA_RANK, GLA_KW), GLA_RANK ** -0.5)
    gla_bg = nrm(ks[16], (DEPTH, GLA_KW), 0.1)
    mla_gq = 1.0 + nrm(ks[17], (DEPTH, MLA_Q_RANK), 0.02)
    mla_wuq = nrm(ks[18], (DEPTH, MLA_Q_RANK, MLA_HEADS * (MLA_NOPE + MLA_ROPE)), MLA_Q_RANK ** -0.5)
    mla_gkv = 1.0 + nrm(ks[19], (DEPTH, MLA_KV_RANK), 0.02)
    mla_wuk = nrm(ks[20], (DEPTH, MLA_KV_RANK, MLA_HEADS * MLA_NOPE), MLA_KV_RANK ** -0.5)
    mla_wuv = nrm(ks[21], (DEPTH, MLA_KV_RANK, MLA_HEADS * MLA_DV), MLA_KV_RANK ** -0.5)
    w_out = nrm(ks[22], (DEPTH, D_MIX, D_MODEL), DEEPNORM_BETA * D_MIX ** -0.5)
    return {"x": x, "c": c, "positions": positions, "w_ada": w_ada, "b_ada": b_ada,
            "ln_g": ln_g, "ln_b": ln_b, "ffn1_wi": ffn1_wi, "ffn1_wo": ffn1_wo,
            "ffn2_wi": ffn2_wi, "ffn2_wo": ffn2_wo, "w_in": w_in, "ml_conv": ml_conv,
            "ml_bi": ml_bi, "ml_bf": ml_bf, "gla_wg": gla_wg, "gla_bg": gla_bg,
            "mla_gq": mla_gq, "mla_wuq": mla_wuq, "mla_gkv": mla_gkv, "mla_wuk": mla_wuk,
            "mla_wuv": mla_wuv, "w_out": w_out}


def reference(x, c, positions, w_ada, b_ada, ln_g, ln_b, ffn1_wi, ffn1_wo, ffn2_wi, ffn2_wo,
              w_in, ml_conv, ml_bi, ml_bf, gla_wg, gla_bg, mla_gq, mla_wuq, mla_gkv,
              mla_wuk, mla_wuv, w_out):
    cos, sin = _rope_tables(positions)
    c_act = jax.nn.silu(c)
    for l in range(DEPTH):
        mod = (c_act @ w_ada[l] + b_ada[l]).reshape(c.shape[0], 3, 3, D_MODEL)

        u = _modulate(x, mod[:, 0])
        r = FFN_RES_WEIGHT * _gate(mod[:, 0], _swiglu(u, ffn1_wi[l], ffn1_wo[l]))
        x = _layer_norm(DEEPNORM_ALPHA * x + r, ln_g[l, 0], ln_b[l, 0])

        u = _modulate(x, mod[:, 1])
        y = _token_mixer(u, cos, sin, w_in[l], ml_conv[l], ml_bi[l], ml_bf[l], gla_wg[l],
                         gla_bg[l], mla_gq[l], mla_wuq[l], mla_gkv[l], mla_wuk[l],
                         mla_wuv[l], w_out[l])
        x = _layer_norm(DEEPNORM_ALPHA * x + _gate(mod[:, 1], y), ln_g[l, 1], ln_b[l, 1])

        u = _modulate(x, mod[:, 2])
        r = FFN_RES_WEIGHT * _gate(mod[:, 2], _swiglu(u, ffn2_wi[l], ffn2_wo[l]))
        x = _layer_norm(DEEPNORM_ALPHA * x + r, ln_g[l, 2], ln_b[l, 2])
    return x
```

```python
import functools

import jax
import jax.numpy as jnp
import numpy as np
from jax import lax
from jax.experimental import pallas as pl
from jax.experimental.pallas import tpu as pltpu

F32 = jnp.float32
BF16 = jnp.bfloat16

ML_HEADS = 4
ML_DH = 128
ML_W = ML_HEADS * ML_DH
ML_CONV = 4
GLA_HEADS = 4
GLA_DK = 64
GLA_DV = 128
GLA_KW = GLA_HEADS * GLA_DK
GLA_VW = GLA_HEADS * GLA_DV
GLA_RANK = 16
GLA_TAU = 16.0
GLA_CHUNK = 64
MLA_HEADS = 8
MLA_NOPE = 128
MLA_ROPE = 64
MLA_HALF = MLA_ROPE // 2
MLA_DV = 128
MLA_QK = MLA_NOPE + MLA_ROPE
MLA_RANK = 512
ROPE_THETA = 10000.0
NORM_EPS = 1e-5
FFN_RES_WEIGHT = 0.5

GRP_A = 4 * ML_W
GRP_B = 2 * GLA_KW + 2 * GLA_VW
GRP_C = 2 * MLA_RANK
GRP_D = 128
D_KR = 0
D_MLI = MLA_ROPE
D_MLF = D_MLI + ML_HEADS
D_GLR = D_MLF + ML_HEADS

VMEM_LIMIT_V7X = 56 * 1024 * 1024

ML_BLOCK = 256
GLA_BLOCK = 256
ATTN_BLOCK = 512


def _cparams(sem):
    return pltpu.CompilerParams(dimension_semantics=sem, vmem_limit_bytes=VMEM_LIMIT_V7X)


def _tile(n, pref):
    t = min(n, pref)
    while n % t:
        t //= 2
    return t


def _sigmoid(x):
    return jax.nn.sigmoid(x)


def _log_sigmoid(x):
    return jnp.minimum(x, 0.0) - jnp.log1p(jnp.exp(-jnp.abs(x)))


def _modulate(x, mod_ref):
    return x * (1.0 + mod_ref[0, 1:2, :]) + mod_ref[0, 0:1, :]


def _residual_layer_norm(x, r, alpha, g_ref, b_ref):
    z = alpha * x + r
    mu = jnp.mean(z, axis=-1, keepdims=True)
    zc = z - mu
    var = jnp.mean(zc * zc, axis=-1, keepdims=True)
    return zc * lax.rsqrt(var + NORM_EPS) * g_ref[...] + b_ref[...]


def _ada_kernel(c_ref, w_ref, b_ref, o_ref):
    c = c_ref[...]
    ca = (c * _sigmoid(c)).astype(BF16)
    o_ref[0] = jnp.dot(ca, w_ref[0].astype(BF16), preferred_element_type=F32) + b_ref[0]


def _ada_mod(c, w_ada, b_ada):
    depth, d, n = w_ada.shape
    bsz = c.shape[0]
    tn = _tile(n, 1024)
    return pl.pallas_call(
        _ada_kernel,
        out_shape=jax.ShapeDtypeStruct((depth, bsz, n), F32),
        grid=(depth, n // tn),
        in_specs=[
            pl.BlockSpec((bsz, d), lambda l, j: (0, 0)),
            pl.BlockSpec((1, d, tn), lambda l, j: (l, 0, j)),
            pl.BlockSpec((1, 1, tn), lambda l, j: (l, 0, j)),
        ],
        out_specs=pl.BlockSpec((1, bsz, tn), lambda l, j: (l, 0, j)),
        compiler_params=_cparams(("parallel", "parallel")),
        name="ada_mod",
    )(c, w_ada, b_ada.reshape(depth, 1, n))


def _rope_kernel(pos_ref, invf_ref, cos_ref, sin_ref):
    ang = pos_ref[0].astype(F32) * invf_ref[...]
    cos_ref[0] = jnp.cos(ang)
    sin_ref[0] = jnp.sin(ang)


def _rope_tables(positions):
    bsz, seq = positions.shape
    ts = _tile(seq, 1024)
    inv_freq = ROPE_THETA ** (-np.arange(MLA_HALF, dtype=np.float32) / MLA_HALF)
    invf = jnp.asarray(np.tile(inv_freq.astype(np.float32), MLA_HEADS)[None, :])
    w = MLA_HEADS * MLA_HALF
    out = jax.ShapeDtypeStruct((bsz, seq, w), F32)
    return pl.pallas_call(
        _rope_kernel,
        out_shape=(out, out),
        grid=(bsz, seq // ts),
        in_specs=[
            pl.BlockSpec((1, ts, 1), lambda b, i: (b, i, 0)),
            pl.BlockSpec((1, w), lambda b, i: (0, 0)),
        ],
        out_specs=(pl.BlockSpec((1, ts, w), lambda b, i: (b, i, 0)),
                   pl.BlockSpec((1, ts, w), lambda b, i: (b, i, 0))),
        compiler_params=_cparams(("parallel", "parallel")),
        name="rope_tables",
    )(positions.reshape(bsz, seq, 1), invf)


def _ffn_kernel(x_ref, mod_ref, wig_ref, wiu_ref, wo_ref, g_ref, b_ref, o_ref,
                u_scr, acc_scr, *, alpha):
    j = pl.program_id(2)

    @pl.when(j == 0)
    def _():
        u_scr[...] = _modulate(x_ref[0], mod_ref).astype(BF16)
        acc_scr[...] = jnp.zeros_like(acc_scr)

    u = u_scr[...]
    g = jnp.dot(u, wig_ref[...], preferred_element_type=F32)
    up = jnp.dot(u, wiu_ref[...], preferred_element_type=F32)
    a = (g * _sigmoid(g) * up).astype(BF16)
    acc_scr[...] += jnp.dot(a, wo_ref[...], preferred_element_type=F32)

    @pl.when(j == pl.num_programs(2) - 1)
    def _():
        r = acc_scr[...] * (FFN_RES_WEIGHT * (1.0 + mod_ref[0, 2:3, :]))
        o_ref[0] = _residual_layer_norm(x_ref[0], r, alpha, g_ref, b_ref)


def _ffn(x, modk, wi, wo, g, b, alpha):
    bsz, seq, d = x.shape
    f = wo.shape[0]
    tm = _tile(seq, 512)
    tf = _tile(f, 512)
    nf = f // tf
    return pl.pallas_call(
        functools.partial(_ffn_kernel, alpha=alpha),
        out_shape=jax.ShapeDtypeStruct(x.shape, F32),
        grid=(bsz, seq // tm, nf),
        in_specs=[
            pl.BlockSpec((1, tm, d), lambda bb, i, j: (bb, i, 0)),
            pl.BlockSpec((1, 3, d), lambda bb, i, j: (bb, 0, 0)),
            pl.BlockSpec((d, tf), lambda bb, i, j: (0, j)),
            pl.BlockSpec((d, tf), lambda bb, i, j: (0, j + nf)),
            pl.BlockSpec((tf, d), lambda bb, i, j: (j, 0)),
            pl.BlockSpec((1, d), lambda bb, i, j: (0, 0)),
            pl.BlockSpec((1, d), lambda bb, i, j: (0, 0)),
        ],
        out_specs=pl.BlockSpec((1, tm, d), lambda bb, i, j: (bb, i, 0)),
        scratch_shapes=[pltpu.VMEM((tm, d), BF16), pltpu.VMEM((tm, d), F32)],
        compiler_params=_cparams(("parallel", "parallel", "arbitrary")),
        name="ffn",
    )(x, modk, wi, wi, wo, g, b)


def _inproj_kernel(x_ref, mod_ref, w_ref, a_ref, b_ref, c_ref, d_ref):
    u = _modulate(x_ref[0], mod_ref).astype(BF16)
    o0, o1, o2, o3 = 0, GRP_A, GRP_A + GRP_B, GRP_A + GRP_B + GRP_C
    a_ref[0] = jnp.dot(u, w_ref[:, o0:o1], preferred_element_type=F32).astype(BF16)
    b_ref[0] = jnp.dot(u, w_ref[:, o1:o2], preferred_element_type=F32).astype(BF16)
    c_ref[0] = jnp.dot(u, w_ref[:, o2:o3], preferred_element_type=F32).astype(BF16)
    d_ref[0] = jnp.dot(u, w_ref[:, o3:o3 + GRP_D], preferred_element_type=F32)


def _inproj(x, modk, w_r):
    bsz, seq, d = x.shape
    tm = _tile(seq, 512)
    n = w_r.shape[1]

    def row(w):
        return pl.BlockSpec((1, tm, w), lambda bb, i: (bb, i, 0))

    return pl.pallas_call(
        _inproj_kernel,
        out_shape=(jax.ShapeDtypeStruct((bsz, seq, GRP_A), BF16),
                   jax.ShapeDtypeStruct((bsz, seq, GRP_B), BF16),
                   jax.ShapeDtypeStruct((bsz, seq, GRP_C), BF16),
                   jax.ShapeDtypeStruct((bsz, seq, GRP_D), F32)),
        grid=(bsz, seq // tm),
        in_specs=[
            row(d),
            pl.BlockSpec((1, 3, d), lambda bb, i: (bb, 0, 0)),
            pl.BlockSpec((d, n), lambda bb, i: (0, 0), pipeline_mode=pl.Buffered(1)),
        ],
        out_specs=(row(GRP_A), row(GRP_B), row(GRP_C), row(GRP_D)),
        compiler_params=_cparams(("parallel", "parallel")),
        name="inproj",
    )(x, modk, w_r)


def _mlstm_kernel(a_ref, d_ref, conv_ref, bias_ref, o_ref, xbuf, c_scr, n_scr, m_scr):
    blk = a_ref.shape[1]
    pad = 8

    @pl.when(pl.program_id(1) == 0)
    def _():
        xbuf[0:pad, :] = jnp.zeros((pad, 2 * ML_W), F32)
        c_scr[...] = jnp.zeros_like(c_scr)
        n_scr[...] = jnp.zeros_like(n_scr)
        m_scr[...] = jnp.zeros_like(m_scr)

    xbuf[pad:pad + blk, :] = a_ref[0, :, 0:2 * ML_W].astype(F32)
    qk = conv_ref[ML_CONV - 1:ML_CONV, :] * xbuf[pad:pad + blk, :]
    for tap in range(ML_CONV - 1):
        sh = ML_CONV - 1 - tap
        qk = qk + conv_ref[tap:tap + 1, :] * xbuf[pad - sh:pad - sh + blk, :]
    qk = qk * _sigmoid(qk)
    xbuf[0:pad, :] = xbuf[blk:blk + pad, :]

    pre = d_ref[0] + bias_ref[...]
    row = lax.broadcasted_iota(jnp.int32, (blk, blk), 0)
    col = lax.broadcasted_iota(jnp.int32, (blk, blk), 1)
    causal = col <= row
    eye = col == row
    tri = jnp.where(causal, 1.0, 0.0).astype(F32)
    bcum = jnp.dot(tri, _log_sigmoid(pre), preferred_element_type=F32,
                   precision=lax.Precision.HIGHEST)

    outs = []
    for h in range(ML_HEADS):
        b_col = bcum[:, D_MLF + h:D_MLF + h + 1]
        r_col = pre[:, D_MLI + h:D_MLI + h + 1] - b_col
        r_row = jnp.sum(jnp.where(eye, r_col, 0.0), axis=0, keepdims=True)
        b_last = b_col[blk - 1:blk, :]
        m_prev = m_scr[h, 0:1, 0:1]
        n_prev = n_scr[h]
        c_prev = c_scr[h]

        q = qk[:, h * ML_DH:(h + 1) * ML_DH]
        k = qk[:, ML_W + h * ML_DH:ML_W + (h + 1) * ML_DH] * (ML_DH ** -0.5)
        v = a_ref[0, :, 2 * ML_W + h * ML_DH:2 * ML_W + (h + 1) * ML_DH]
        og = a_ref[0, :, 3 * ML_W + h * ML_DH:3 * ML_W + (h + 1) * ML_DH].astype(F32)
        qb = q.astype(BF16)

        dmat = jnp.where(causal, b_col + r_row, -jnp.inf)
        m_inter = b_col + m_prev
        m_t = jnp.maximum(m_inter, jnp.max(dmat, axis=1, keepdims=True))
        s = lax.dot_general(qb, k.astype(BF16), (((1,), (1,)), ((), ())),
                            preferred_element_type=F32) * jnp.exp(dmat - m_t)
        inter = jnp.exp(m_inter - m_t)
        num = (jnp.dot(s.astype(BF16), v, preferred_element_type=F32)
               + inter * jnp.dot(qb, c_prev.astype(BF16), preferred_element_type=F32))
        den = (jnp.sum(s, axis=1, keepdims=True)
               + inter * jnp.sum(q * n_prev, axis=1, keepdims=True))
        hh = num / jnp.maximum(jnp.abs(den), jnp.exp(-m_t))
        mu = jnp.mean(hh, axis=-1, keepdims=True)
        hc = hh - mu
        var = jnp.mean(hc * hc, axis=-1, keepdims=True)
        outs.append(_sigmoid(og) * (hc * lax.rsqrt(var + NORM_EPS)))

        a_col = b_last + r_col
        a_max = jnp.max(a_col, axis=0, keepdims=True)
        kw = k * jnp.exp(a_col - a_max)
        c_loc = lax.dot_general(kw.astype(BF16), v, (((0,), (0,)), ((), ())),
                                preferred_element_type=F32)
        n_loc = jnp.sum(kw, axis=0, keepdims=True)
        m_new = jnp.maximum(b_last + m_prev, a_max)
        dec = jnp.exp(b_last + m_prev - m_new)
        inj = jnp.exp(a_max - m_new)
        c_scr[h] = dec * c_prev + inj * c_loc
        n_scr[h] = dec * n_prev + inj * n_loc
        m_scr[h] = jnp.broadcast_to(m_new, m_scr.shape[1:])

    o_ref[0] = jnp.concatenate(outs, axis=-1).astype(o_ref.dtype)


def _mlstm(a, dg, conv, bias_row):
    bsz, seq, _ = a.shape
    blk = _tile(seq, ML_BLOCK)
    return pl.pallas_call(
        _mlstm_kernel,
        out_shape=jax.ShapeDtypeStruct((bsz, seq, ML_W), BF16),
        grid=(bsz, seq // blk),
        in_specs=[
            pl.BlockSpec((1, blk, GRP_A), lambda bb, c: (bb, c, 0)),
            pl.BlockSpec((1, blk, GRP_D), lambda bb, c: (bb, c, 0)),
            pl.BlockSpec((ML_CONV, 2 * ML_W), lambda bb, c: (0, 0)),
            pl.BlockSpec((1, GRP_D), lambda bb, c: (0, 0)),
        ],
        out_specs=pl.BlockSpec((1, blk, ML_W), lambda bb, c: (bb, c, 0)),
        scratch_shapes=[
            pltpu.VMEM((blk + 8, 2 * ML_W), F32),
            pltpu.VMEM((ML_HEADS, ML_DH, ML_DH), F32),
            pltpu.VMEM((ML_HEADS, 1, ML_DH), F32),
            pltpu.VMEM((ML_HEADS, 1, ML_DH), F32),
        ],
        compiler_params=_cparams(("parallel", "arbitrary")),
        name="mlstm",
    )(a, dg, conv, bias_row)


def _gla_kernel(b_ref, d_ref, wg_ref, bg_ref, o_ref, st_scr):
    blk = b_ref.shape[1]
    ch = GLA_CHUNK

    @pl.when(pl.program_id(1) == 0)
    def _():
        st_scr[...] = jnp.zeros_like(st_scr)

    gate_pre = jnp.dot(d_ref[0].astype(BF16), wg_ref[...],
                       preferred_element_type=F32) + bg_ref[...]
    log_a = _log_sigmoid(gate_pre) / GLA_TAU
    row = lax.broadcasted_iota(jnp.int32, (blk, blk), 0)
    col = lax.broadcasted_iota(jnp.int32, (blk, blk), 1)
    same_chunk = (row // ch) == (col // ch)
    tri = jnp.where((col <= row) & same_chunk, 1.0, 0.0).astype(F32)
    bc = jnp.dot(tri, log_a, preferred_element_type=F32,
                 precision=lax.Precision.HIGHEST)
    e_pos = jnp.exp(bc)
    e_neg = jnp.exp(-bc)
    crow = lax.broadcasted_iota(jnp.int32, (ch, ch), 0)
    ccol = lax.broadcasted_iota(jnp.int32, (ch, ch), 1)
    causal = ccol <= crow

    for sc in range(blk // ch):
        r0, r1 = sc * ch, (sc + 1) * ch
        outs = []
        for h in range(GLA_HEADS):
            k0, k1 = h * GLA_DK, (h + 1) * GLA_DK
            q = b_ref[0, r0:r1, k0:k1].astype(F32) * (GLA_DK ** -0.5)
            k = b_ref[0, r0:r1, GLA_KW + k0:GLA_KW + k1].astype(F32)
            v = b_ref[0, r0:r1, 2 * GLA_KW + h * GLA_DV:2 * GLA_KW + (h + 1) * GLA_DV]
            rg = b_ref[0, r0:r1, 2 * GLA_KW + GLA_VW + h * GLA_DV:
                       2 * GLA_KW + GLA_VW + (h + 1) * GLA_DV].astype(F32)
            bc_h = bc[r0:r1, k0:k1]
            b_last = bc_h[ch - 1:ch, :]
            q_dec = (q * e_pos[r0:r1, k0:k1]).astype(BF16)
            k_inv = (k * e_neg[r0:r1, k0:k1]).astype(BF16)
            attn = lax.dot_general(q_dec, k_inv, (((1,), (1,)), ((), ())),
                                   preferred_element_type=F32)
            attn = jnp.where(causal, attn, 0.0)
            st = st_scr[h]
            o = (jnp.dot(attn.astype(BF16), v, preferred_element_type=F32)
                 + lax.dot_general(q_dec, st.astype(BF16), (((1,), (1,)), ((), ())),
                                   preferred_element_type=F32))
            k_w = (k * jnp.exp(b_last - bc_h)).astype(BF16)
            s_loc_t = lax.dot_general(v, k_w, (((0,), (0,)), ((), ())),
                                      preferred_element_type=F32)
            st_scr[h] = st * jnp.exp(b_last) + s_loc_t
            on = o * lax.rsqrt(jnp.mean(o * o, axis=-1, keepdims=True) + NORM_EPS)
            outs.append(rg * _sigmoid(rg) * on)
        o_ref[0, r0:r1, :] = jnp.concatenate(outs, axis=-1).astype(o_ref.dtype)


def _gla(bg, dg, wg_pad, gla_bg):
    bsz, seq, _ = bg.shape
    blk = _tile(seq, GLA_BLOCK)
    return pl.pallas_call(
        _gla_kernel,
        out_shape=jax.ShapeDtypeStruct((bsz, seq, GLA_VW), BF16),
        grid=(bsz, seq // blk),
        in_specs=[
            pl.BlockSpec((1, blk, GRP_B), lambda bb, c: (bb, c, 0)),
            pl.BlockSpec((1, blk, GRP_D), lambda bb, c: (bb, c, 0)),
            pl.BlockSpec((GRP_D, GLA_KW), lambda bb, c: (0, 0)),
            pl.BlockSpec((1, GLA_KW), lambda bb, c: (0, 0)),
        ],
        out_specs=pl.BlockSpec((1, blk, GLA_VW), lambda bb, c: (bb, c, 0)),
        scratch_shapes=[pltpu.VMEM((GLA_HEADS, GLA_DV, GLA_DK), F32)],
        compiler_params=_cparams(("parallel", "arbitrary")),
        name="gla",
    )(bg, dg, wg_pad, gla_bg)


def _rms(x, g_ref):
    return x * lax.rsqrt(jnp.mean(x * x, axis=-1, keepdims=True) + NORM_EPS) * g_ref[...]


def _mla_proj_kernel(c_ref, d_ref, cos_ref, sin_ref, gq_ref, gkv_ref, wqn_ref, wq1_ref,
                     wq2_ref, wuk_ref, wuv_ref, q_ref, k_ref, v_ref):
    cq = c_ref[0, :, 0:MLA_RANK].astype(F32)
    ckv = c_ref[0, :, MLA_RANK:2 * MLA_RANK].astype(F32)
    qn = _rms(cq, gq_ref).astype(BF16)
    kvn = _rms(ckv, gkv_ref).astype(BF16)
    cos = cos_ref[0]
    sin = sin_ref[0]
    q_nope = jnp.dot(qn, wqn_ref[...], preferred_element_type=F32)
    q1 = jnp.dot(qn, wq1_ref[...], preferred_element_type=F32)
    q2 = jnp.dot(qn, wq2_ref[...], preferred_element_type=F32)
    qr1 = q1 * cos - q2 * sin
    qr2 = q1 * sin + q2 * cos
    k_nope = jnp.dot(kvn, wuk_ref[...], preferred_element_type=F32)
    vv = jnp.dot(kvn, wuv_ref[...], preferred_element_type=F32)
    kr = d_ref[0]
    k1 = kr[:, D_KR:D_KR + MLA_HALF]
    k2 = kr[:, D_KR + MLA_HALF:D_KR + MLA_ROPE]
    c1 = cos[:, 0:MLA_HALF]
    s1 = sin[:, 0:MLA_HALF]
    kr1 = (k1 * c1 - k2 * s1).astype(BF16)
    kr2 = (k1 * s1 + k2 * c1).astype(BF16)
    for h in range(MLA_HEADS):
        n0, n1 = h * MLA_NOPE, (h + 1) * MLA_NOPE
        h0, h1 = h * MLA_HALF, (h + 1) * MLA_HALF
        q_ref[0, h] = jnp.concatenate(
            [q_nope[:, n0:n1], qr1[:, h0:h1], qr2[:, h0:h1]], axis=-1).astype(BF16)
        k_ref[0, h] = jnp.concatenate(
            [k_nope[:, n0:n1].astype(BF16), kr1, kr2], axis=-1)
        v_ref[0, h] = vv[:, h * MLA_DV:(h + 1) * MLA_DV].astype(BF16)


def _mla_proj(cg, dg, cos8, sin8, gq, gkv, wqn, wq1, wq2, wuk, wuv):
    bsz, seq, _ = cg.shape
    tm = _tile(seq, 512)
    rw = MLA_HEADS * MLA_HALF

    def row(w):
        return pl.BlockSpec((1, tm, w), lambda bb, i: (bb, i, 0))

    def full(arr):
        return pl.BlockSpec(arr.shape, lambda bb, i: (0,) * arr.ndim)

    def heads(w):
        return pl.BlockSpec((1, MLA_HEADS, tm, w), lambda bb, i: (bb, 0, i, 0))

    return pl.pallas_call(
        _mla_proj_kernel,
        out_shape=(jax.ShapeDtypeStruct((bsz, MLA_HEADS, seq, MLA_QK), BF16),
                   jax.ShapeDtypeStruct((bsz, MLA_HEADS, seq, MLA_QK), BF16),
                   jax.ShapeDtypeStruct((bsz, MLA_HEADS, seq, MLA_DV), BF16)),
        grid=(bsz, seq // tm),
        in_specs=[row(GRP_C), row(GRP_D), row(rw), row(rw), full(gq), full(gkv),
                  full(wqn), full(wq1), full(wq2), full(wuk), full(wuv)],
        out_specs=(heads(MLA_QK), heads(MLA_QK), heads(MLA_DV)),
        compiler_params=_cparams(("parallel", "parallel")),
        name="mla_proj",
    )(cg, dg, cos8, sin8, gq, gkv, wqn, wq1, wq2, wuk, wuv)


def _attn_kernel(q_ref, k_ref, v_ref, o_ref, *, scale):
    tq = q_ref.shape[2]
    i = pl.program_id(2)
    q = q_ref[0, 0]

    def scores(j):
        start = pl.multiple_of(j * tq, tq)
        k = k_ref[0, 0, pl.ds(start, tq), :]
        v = v_ref[0, 0, pl.ds(start, tq), :]
        s = lax.dot_general(q, k, (((1,), (1,)), ((), ())), preferred_element_type=F32) * scale
        return s, v

    def update(carry, s, v):
        m, l, acc = carry
        m_new = jnp.maximum(m, jnp.max(s, axis=1, keepdims=True))
        p = jnp.exp(s - m_new)
        corr = jnp.exp(m - m_new)
        l = corr * l + jnp.sum(p, axis=1, keepdims=True)
        acc = corr * acc + jnp.dot(p.astype(BF16), v, preferred_element_type=F32)
        return m_new, l, acc

    def body(j, carry):
        s, v = scores(j)
        return update(carry, s, v)

    init = (jnp.full((tq, 1), -jnp.inf, F32), jnp.zeros((tq, 1), F32),
            jnp.zeros((tq, MLA_DV), F32))
    carry = lax.fori_loop(0, i, body, init)
    s, v = scores(i)
    row = lax.broadcasted_iota(jnp.int32, (tq, tq), 0)
    col = lax.broadcasted_iota(jnp.int32, (tq, tq), 1)
    s = jnp.where(col <= row, s, -jnp.inf)
    _, l, acc = update(carry, s, v)
    o_ref[0] = (acc / l).astype(o_ref.dtype)


def _attention(q, k, v):
    bsz, nh, seq, _ = q.shape
    tq = _tile(seq, ATTN_BLOCK)
    return pl.pallas_call(
        functools.partial(_attn_kernel, scale=MLA_QK ** -0.5),
        out_shape=jax.ShapeDtypeStruct((bsz, seq, nh * MLA_DV), BF16),
        grid=(bsz, nh, seq // tq),
        in_specs=[
            pl.BlockSpec((1, 1, tq, MLA_QK), lambda bb, h, i: (bb, h, i, 0)),
            pl.BlockSpec((1, 1, seq, MLA_QK), lambda bb, h, i: (bb, h, 0, 0)),
            pl.BlockSpec((1, 1, seq, MLA_DV), lambda bb, h, i: (bb, h, 0, 0)),
        ],
        out_specs=pl.BlockSpec((1, tq, MLA_DV), lambda bb, h, i: (bb, i, h)),
        compiler_params=_cparams(("parallel", "parallel", "arbitrary")),
        name="mla_attn",
    )(q, k, v)


def _outproj_kernel(x_ref, mod_ref, yml_ref, ygla_ref, ymla_ref, w_ref, g_ref, b_ref, o_ref,
                    *, alpha):
    y = jnp.concatenate([yml_ref[0], ygla_ref[0], ymla_ref[0]], axis=-1)
    r = jnp.dot(y, w_ref[...], preferred_element_type=F32) * (1.0 + mod_ref[0, 2:3, :])
    o_ref[0] = _residual_layer_norm(x_ref[0], r, alpha, g_ref, b_ref)


def _outproj(x, modk, y_ml, y_gla, y_mla, w_out, g, b, alpha):
    bsz, seq, d = x.shape
    tm = _tile(seq, 512)

    def row(w):
        return pl.BlockSpec((1, tm, w), lambda bb, i: (bb, i, 0))

    return pl.pallas_call(
        functools.partial(_outproj_kernel, alpha=alpha),
        out_shape=jax.ShapeDtypeStruct(x.shape, F32),
        grid=(bsz, seq // tm),
        in_specs=[
            row(d),
            pl.BlockSpec((1, 3, d), lambda bb, i: (bb, 0, 0)),
            row(y_ml.shape[-1]), row(y_gla.shape[-1]), row(y_mla.shape[-1]),
            pl.BlockSpec(w_out.shape, lambda bb, i: (0, 0), pipeline_mode=pl.Buffered(1)),
            pl.BlockSpec((1, d), lambda bb, i: (0, 0)),
            pl.BlockSpec((1, d), lambda bb, i: (0, 0)),
        ],
        out_specs=row(d),
        compiler_params=_cparams(("parallel", "parallel")),
        name="outproj",
    )(x, modk, y_ml, y_gla, y_mla, w_out, g, b)


def _regroup_w_in(w_in):
    sizes = (ML_W, ML_W, ML_W, ML_HEADS, ML_HEADS, ML_W,
             GLA_KW, GLA_KW, GLA_VW, GLA_RANK, GLA_VW,
             MLA_RANK, MLA_RANK, MLA_ROPE)
    offs = np.concatenate([[0], np.cumsum(sizes)])
    (ml_q, ml_k, ml_v, ml_i, ml_f, ml_o, gl_q, gl_k, gl_v, gl_lr, gl_r, c_q, c_kv, k_r) = (
        w_in[..., int(offs[n]):int(offs[n + 1])] for n in range(len(sizes)))
    used = MLA_ROPE + 2 * ML_HEADS + GLA_RANK
    pad = jnp.zeros(w_in.shape[:-1] + (GRP_D - used,), w_in.dtype)
    return jnp.concatenate([ml_q, ml_k, ml_v, ml_o, gl_q, gl_k, gl_v, gl_r, c_q, c_kv,
                            k_r, ml_i, ml_f, gl_lr, pad], axis=-1).astype(BF16)


def kernel(x, c, positions, w_ada, b_ada, ln_g, ln_b, ffn1_wi, ffn1_wo, ffn2_wi, ffn2_wo,
           w_in, ml_conv, ml_bi, ml_bf, gla_wg, gla_bg, mla_gq, mla_wuq, mla_gkv,
           mla_wuk, mla_wuv, w_out):
    depth = w_ada.shape[0]
    bsz, seq, d = x.shape
    alpha = (2.0 * depth) ** 0.25

    mod = _ada_mod(c, w_ada, b_ada).reshape(depth, bsz, 3, 3, d)
    cos8, sin8 = _rope_tables(positions)

    w_in_r = _regroup_w_in(w_in)
    ffn1_wi_b, ffn1_wo_b = ffn1_wi.astype(BF16), ffn1_wo.astype(BF16)
    ffn2_wi_b, ffn2_wo_b = ffn2_wi.astype(BF16), ffn2_wo.astype(BF16)
    w_out_b = w_out.astype(BF16)
    wuq = mla_wuq.reshape(depth, MLA_RANK, MLA_HEADS, MLA_QK)
    wqn = wuq[..., :MLA_NOPE].reshape(depth, MLA_RANK, MLA_HEADS * MLA_NOPE).astype(BF16)
    wq1 = wuq[..., MLA_NOPE:MLA_NOPE + MLA_HALF].reshape(depth, MLA_RANK, -1).astype(BF16)
    wq2 = wuq[..., MLA_NOPE + MLA_HALF:].reshape(depth, MLA_RANK, -1).astype(BF16)
    wuk_b, wuv_b = mla_wuk.astype(BF16), mla_wuv.astype(BF16)
    gate_bias = jnp.zeros((depth, 1, GRP_D), F32)
    gate_bias = gate_bias.at[:, 0, D_MLI:D_MLI + ML_HEADS].set(ml_bi)
    gate_bias = gate_bias.at[:, 0, D_MLF:D_MLF + ML_HEADS].set(ml_bf)
    wg_pad = jnp.zeros((depth, GRP_D, GLA_KW), F32)
    wg_pad = wg_pad.at[:, D_GLR:D_GLR + GLA_RANK, :].set(gla_wg).astype(BF16)

    for l in range(depth):
        def lnp(k):
            return ln_g[l, k][None, :], ln_b[l, k][None, :]

        g0, b0 = lnp(0)
        x = _ffn(x, mod[l, :, 0], ffn1_wi_b[l], ffn1_wo_b[l], g0, b0, alpha)

        ag, bg, cg, dg = _inproj(x, mod[l, :, 1], w_in_r[l])
        y_ml = _mlstm(ag, dg, ml_conv[l], gate_bias[l])
        y_gla = _gla(bg, dg, wg_pad[l], gla_bg[l][None, :])
        q, k, v = _mla_proj(cg, dg, cos8, sin8, mla_gq[l][None, :], mla_gkv[l][None, :],
                            wqn[l], wq1[l], wq2[l], wuk_b[l], wuv_b[l])
        y_mla = _attention(q, k, v)
        g1, b1 = lnp(1)
        x = _outproj(x, mod[l, :, 1], y_ml, y_gla, y_mla, w_out_b[l], g1, b1, alpha)

        g2, b2 = lnp(2)
        x = _ffn(x, mod[l, :, 2], ffn2_wi_b[l], ffn2_wo_b[l], g2, b2, alpha)
    return x
```

```python
import functools
import math

import jax
import jax.numpy as jnp
import numpy as np
from jax import lax
from jax.experimental import pallas as pl
from jax.experimental.pallas import tpu as pltpu

F32 = jnp.float32
BF16 = jnp.bfloat16

ML_HEADS = 4
ML_DH = 128
ML_W = ML_HEADS * ML_DH
ML_CONV = 4
GLA_HEADS = 4
GLA_DK = 64
GLA_DV = 128
GLA_KW = GLA_HEADS * GLA_DK
GLA_VW = GLA_HEADS * GLA_DV
GLA_RANK = 16
GLA_TAU = 16.0
GLA_CHUNK = 64
MLA_HEADS = 8
MLA_NOPE = 128
MLA_ROPE = 64
MLA_HALF = MLA_ROPE // 2
MLA_DV = 128
MLA_QK = MLA_NOPE + MLA_ROPE
MLA_RANK = 512
ROPE_THETA = 10000.0
NORM_EPS = 1e-5
FFN_RES_WEIGHT = 0.5
QK_SCALE_LOG2 = math.log2(math.e) * MLA_QK ** -0.5

GRP_A = 4 * ML_W
GRP_B = 2 * GLA_KW + 2 * GLA_VW
GRP_C = 2 * MLA_RANK
GRP_D = 128
D_KR = 0
D_MLI = MLA_ROPE
D_MLF = D_MLI + ML_HEADS
D_GLR = D_MLF + ML_HEADS

VMEM_LIMIT_V7X = 56 * 1024 * 1024

ML_BLOCK = 256
GLA_BLOCK = 256
ATTN_BLOCK = 512


def _cparams(sem):
    return pltpu.CompilerParams(dimension_semantics=sem, vmem_limit_bytes=VMEM_LIMIT_V7X)


def _tile(n, pref):
    t = min(n, pref)
    while n % t:
        t //= 2
    return t


def _layer_spec(arr, l, nargs):
    shape = (None,) + arr.shape[1:]
    zeros = (0,) * (arr.ndim - 1)
    return pl.BlockSpec(shape, lambda *_: (l,) + zeros)


def _mod_spec(mod, l, k):
    d = mod.shape[-1]
    return pl.BlockSpec((None, 1, None, 3, d), lambda bb, *_: (l, bb, k, 0, 0))


def _ln_spec(ln, l, k):
    return pl.BlockSpec((None, None, 1, ln.shape[-1]), lambda *_: (l, k, 0, 0))


def _sigmoid(x):
    return jax.nn.sigmoid(x)


def _log_sigmoid(x):
    return jnp.minimum(x, 0.0) - jnp.log1p(jnp.exp(-jnp.abs(x)))


def _modulate(x, mod_ref):
    return x * (1.0 + mod_ref[0, 1:2, :]) + mod_ref[0, 0:1, :]


def _residual_layer_norm(x, r, alpha, g_ref, b_ref):
    z = alpha * x + r
    mu = jnp.mean(z, axis=-1, keepdims=True)
    zc = z - mu
    var = jnp.mean(zc * zc, axis=-1, keepdims=True)
    return zc * lax.rsqrt(var + NORM_EPS) * g_ref[...] + b_ref[...]


def _ada_kernel(c_ref, w_ref, b_ref, o_ref):
    c = c_ref[...]
    ca = (c * _sigmoid(c)).astype(BF16)
    o_ref[0] = jnp.dot(ca, w_ref[0].astype(BF16), preferred_element_type=F32) + b_ref[0]


def _ada_mod(c, w_ada, b_ada):
    depth, d, n = w_ada.shape
    bsz = c.shape[0]
    tn = _tile(n, 1024)
    return pl.pallas_call(
        _ada_kernel,
        out_shape=jax.ShapeDtypeStruct((depth, bsz, n), F32),
        grid=(depth, n // tn),
        in_specs=[
            pl.BlockSpec((bsz, d), lambda l, j: (0, 0)),
            pl.BlockSpec((1, d, tn), lambda l, j: (l, 0, j)),
            pl.BlockSpec((1, 1, tn), lambda l, j: (l, 0, j)),
        ],
        out_specs=pl.BlockSpec((1, bsz, tn), lambda l, j: (l, 0, j)),
        compiler_params=_cparams(("parallel", "parallel")),
        name="ada_mod",
    )(c, w_ada, b_ada.reshape(depth, 1, n))


def _rope_kernel(pcol_ref, prow_ref, frow_ref, fcol_ref, cos_ref, sin_ref, cost_ref, sint_ref):
    ang = pcol_ref[0].astype(F32) * frow_ref[...]
    cos_ref[0] = jnp.cos(ang)
    sin_ref[0] = jnp.sin(ang)
    ang_t = fcol_ref[...] * prow_ref[0].astype(F32)
    cost_ref[0] = jnp.cos(ang_t)
    sint_ref[0] = jnp.sin(ang_t)


def _rope_tables(positions):
    bsz, seq = positions.shape
    ts = _tile(seq, 1024)
    inv_freq = (ROPE_THETA ** (-np.arange(MLA_HALF, dtype=np.float32) / MLA_HALF)).astype(np.float32)
    nat = jax.ShapeDtypeStruct((bsz, seq, MLA_HALF), F32)
    tra = jax.ShapeDtypeStruct((bsz, MLA_HALF, seq), F32)
    return pl.pallas_call(
        _rope_kernel,
        out_shape=(nat, nat, tra, tra),
        grid=(bsz, seq // ts),
        in_specs=[
            pl.BlockSpec((1, ts, 1), lambda b, i: (b, i, 0)),
            pl.BlockSpec((1, 1, ts), lambda b, i: (b, 0, i)),
            pl.BlockSpec((1, MLA_HALF), lambda b, i: (0, 0)),
            pl.BlockSpec((MLA_HALF, 1), lambda b, i: (0, 0)),
        ],
        out_specs=(pl.BlockSpec((1, ts, MLA_HALF), lambda b, i: (b, i, 0)),
                   pl.BlockSpec((1, ts, MLA_HALF), lambda b, i: (b, i, 0)),
                   pl.BlockSpec((1, MLA_HALF, ts), lambda b, i: (b, 0, i)),
                   pl.BlockSpec((1, MLA_HALF, ts), lambda b, i: (b, 0, i))),
        compiler_params=_cparams(("parallel", "parallel")),
        name="rope_tables",
    )(positions.reshape(bsz, seq, 1), positions.reshape(bsz, 1, seq),
      jnp.asarray(inv_freq[None, :]), jnp.asarray(inv_freq[:, None]))


def _ffn_kernel(x_ref, mod_ref, wig_ref, wiu_ref, wo_ref, g_ref, b_ref, o_ref,
                u_scr, acc_scr, *, alpha):
    j = pl.program_id(2)

    @pl.when(j == 0)
    def _():
        u_scr[...] = _modulate(x_ref[0], mod_ref).astype(BF16)
        acc_scr[...] = jnp.zeros_like(acc_scr)

    u = u_scr[...]
    g = jnp.dot(u, wig_ref[...], preferred_element_type=F32)
    up = jnp.dot(u, wiu_ref[...], preferred_element_type=F32)
    a = (g * _sigmoid(g) * up).astype(BF16)
    acc_scr[...] += jnp.dot(a, wo_ref[...], preferred_element_type=F32)

    @pl.when(j == pl.num_programs(2) - 1)
    def _():
        r = acc_scr[...] * (FFN_RES_WEIGHT * (1.0 + mod_ref[0, 2:3, :]))
        o_ref[0] = _residual_layer_norm(x_ref[0], r, alpha, g_ref, b_ref)


def _ffn(x, mod, wi, wo, ln_g, ln_b, l, k, alpha):
    bsz, seq, d = x.shape
    f = wo.shape[1]
    tm = _tile(seq, 512)
    tf = _tile(f, 512)
    nf = f // tf
    return pl.pallas_call(
        functools.partial(_ffn_kernel, alpha=alpha),
        out_shape=jax.ShapeDtypeStruct(x.shape, F32),
        grid=(bsz, seq // tm, nf),
        in_specs=[
            pl.BlockSpec((1, tm, d), lambda bb, i, j: (bb, i, 0)),
            _mod_spec(mod, l, k),
            pl.BlockSpec((None, d, tf), lambda bb, i, j: (l, 0, j)),
            pl.BlockSpec((None, d, tf), lambda bb, i, j: (l, 0, j + nf)),
            pl.BlockSpec((None, tf, d), lambda bb, i, j: (l, j, 0)),
            _ln_spec(ln_g, l, k),
            _ln_spec(ln_b, l, k),
        ],
        out_specs=pl.BlockSpec((1, tm, d), lambda bb, i, j: (bb, i, 0)),
        scratch_shapes=[pltpu.VMEM((tm, d), BF16), pltpu.VMEM((tm, d), F32)],
        compiler_params=_cparams(("parallel", "parallel", "arbitrary")),
        name="ffn",
    )(x, mod, wi, wi, wo, ln_g, ln_b)


def _inproj_kernel(x_ref, mod_ref, w_ref, a_ref, b_ref, c_ref, d_ref):
    u = _modulate(x_ref[0], mod_ref).astype(BF16)
    o0, o1, o2, o3 = 0, GRP_A, GRP_A + GRP_B, GRP_A + GRP_B + GRP_C
    a_ref[0] = jnp.dot(u, w_ref[:, o0:o1], preferred_element_type=F32).astype(BF16)
    b_ref[0] = jnp.dot(u, w_ref[:, o1:o2], preferred_element_type=F32).astype(BF16)
    c_ref[0] = jnp.dot(u, w_ref[:, o2:o3], preferred_element_type=F32).astype(BF16)
    d_ref[0] = jnp.dot(u, w_ref[:, o3:o3 + GRP_D], preferred_element_type=F32)


def _inproj(x, mod, w_r, l):
    bsz, seq, d = x.shape
    tm = _tile(seq, 512)
    n = w_r.shape[-1]

    def row(w):
        return pl.BlockSpec((1, tm, w), lambda bb, i: (bb, i, 0))

    return pl.pallas_call(
        _inproj_kernel,
        out_shape=(jax.ShapeDtypeStruct((bsz, seq, GRP_A), BF16),
                   jax.ShapeDtypeStruct((bsz, seq, GRP_B), BF16),
                   jax.ShapeDtypeStruct((bsz, seq, GRP_C), BF16),
                   jax.ShapeDtypeStruct((bsz, seq, GRP_D), F32)),
        grid=(bsz, seq // tm),
        in_specs=[
            row(d),
            _mod_spec(mod, l, 1),
            pl.BlockSpec((None, d, n), lambda bb, i: (l, 0, 0), pipeline_mode=pl.Buffered(1)),
        ],
        out_specs=(row(GRP_A), row(GRP_B), row(GRP_C), row(GRP_D)),
        compiler_params=_cparams(("parallel", "parallel")),
        name="inproj",
    )(x, mod, w_r)


def _mlstm_kernel(a_ref, d_ref, conv_ref, bias_ref, o_ref, xbuf, c_scr, n_scr, m_scr):
    blk = a_ref.shape[1]
    pad = 8

    @pl.when(pl.program_id(1) == 0)
    def _():
        xbuf[0:pad, :] = jnp.zeros((pad, 2 * ML_W), F32)
        c_scr[...] = jnp.zeros_like(c_scr)
        n_scr[...] = jnp.zeros_like(n_scr)
        m_scr[...] = jnp.zeros_like(m_scr)

    xbuf[pad:pad + blk, :] = a_ref[0, :, 0:2 * ML_W].astype(F32)
    qk = conv_ref[ML_CONV - 1:ML_CONV, :] * xbuf[pad:pad + blk, :]
    for tap in range(ML_CONV - 1):
        sh = ML_CONV - 1 - tap
        qk = qk + conv_ref[tap:tap + 1, :] * xbuf[pad - sh:pad - sh + blk, :]
    qk = qk * _sigmoid(qk)
    xbuf[0:pad, :] = xbuf[blk:blk + pad, :]

    pre = d_ref[0] + bias_ref[...]
    row = lax.broadcasted_iota(jnp.int32, (blk, blk), 0)
    col = lax.broadcasted_iota(jnp.int32, (blk, blk), 1)
    causal = col <= row
    eye = col == row
    tri = jnp.where(causal, 1.0, 0.0).astype(F32)
    bcum = jnp.dot(tri, _log_sigmoid(pre), preferred_element_type=F32,
                   precision=lax.Precision.HIGHEST)

    outs = []
    for h in range(ML_HEADS):
        b_col = bcum[:, D_MLF + h:D_MLF + h + 1]
        r_col = pre[:, D_MLI + h:D_MLI + h + 1] - b_col
        r_row = jnp.sum(jnp.where(eye, r_col, 0.0), axis=0, keepdims=True)
        b_last = b_col[blk - 1:blk, :]
        m_prev = m_scr[h, 0:1, 0:1]
        n_prev = n_scr[h]
        c_prev = c_scr[h]

        q = qk[:, h * ML_DH:(h + 1) * ML_DH]
        k = qk[:, ML_W + h * ML_DH:ML_W + (h + 1) * ML_DH] * (ML_DH ** -0.5)
        v = a_ref[0, :, 2 * ML_W + h * ML_DH:2 * ML_W + (h + 1) * ML_DH]
        og = a_ref[0, :, 3 * ML_W + h * ML_DH:3 * ML_W + (h + 1) * ML_DH].astype(F32)
        qb = q.astype(BF16)

        dmat = jnp.where(causal, b_col + r_row, -jnp.inf)
        m_inter = b_col + m_prev
        m_t = jnp.maximum(m_inter, jnp.max(dmat, axis=1, keepdims=True))
        s = lax.dot_general(qb, k.astype(BF16), (((1,), (1,)), ((), ())),
                            preferred_element_type=F32) * jnp.exp(dmat - m_t)
        inter = jnp.exp(m_inter - m_t)
        num = (jnp.dot(s.astype(BF16), v, preferred_element_type=F32)
               + inter * jnp.dot(qb, c_prev.astype(BF16), preferred_element_type=F32))
        den = (jnp.sum(s, axis=1, keepdims=True)
               + inter * jnp.sum(q * n_prev, axis=1, keepdims=True))
        hh = num / jnp.maximum(jnp.abs(den), jnp.exp(-m_t))
        mu = jnp.mean(hh, axis=-1, keepdims=True)
        hc = hh - mu
        var = jnp.mean(hc * hc, axis=-1, keepdims=True)
        outs.append(_sigmoid(og) * (hc * lax.rsqrt(var + NORM_EPS)))

        a_col = b_last + r_col
        a_max = jnp.max(a_col, axis=0, keepdims=True)
        kw = k * jnp.exp(a_col - a_max)
        c_loc = lax.dot_general(kw.astype(BF16), v, (((0,), (0,)), ((), ())),
                                preferred_element_type=F32)
        n_loc = jnp.sum(kw, axis=0, keepdims=True)
        m_new = jnp.maximum(b_last + m_prev, a_max)
        dec = jnp.exp(b_last + m_prev - m_new)
        inj = jnp.exp(a_max - m_new)
        c_scr[h] = dec * c_prev + inj * c_loc
        n_scr[h] = dec * n_prev + inj * n_loc
        m_scr[h] = jnp.broadcast_to(m_new, m_scr.shape[1:])

    o_ref[0] = jnp.concatenate(outs, axis=-1).astype(o_ref.dtype)


def _mlstm(a, dg, conv, gate_bias, l):
    bsz, seq, _ = a.shape
    blk = _tile(seq, ML_BLOCK)
    return pl.pallas_call(
        _mlstm_kernel,
        out_shape=jax.ShapeDtypeStruct((bsz, seq, ML_W), BF16),
        grid=(bsz, seq // blk),
        in_specs=[
            pl.BlockSpec((1, blk, GRP_A), lambda bb, c: (bb, c, 0)),
            pl.BlockSpec((1, blk, GRP_D), lambda bb, c: (bb, c, 0)),
            _layer_spec(conv, l, 2),
            _layer_spec(gate_bias, l, 2),
        ],
        out_specs=pl.BlockSpec((1, blk, ML_W), lambda bb, c: (bb, c, 0)),
        scratch_shapes=[
            pltpu.VMEM((blk + 8, 2 * ML_W), F32),
            pltpu.VMEM((ML_HEADS, ML_DH, ML_DH), F32),
            pltpu.VMEM((ML_HEADS, 1, ML_DH), F32),
            pltpu.VMEM((ML_HEADS, 1, ML_DH), F32),
        ],
        compiler_params=_cparams(("parallel", "arbitrary")),
        name="mlstm",
    )(a, dg, conv, gate_bias)


def _gla_kernel(b_ref, d_ref, wg_ref, bg_ref, o_ref, st_scr):
    blk = b_ref.shape[1]
    ch = GLA_CHUNK

    @pl.when(pl.program_id(1) == 0)
    def _():
        st_scr[...] = jnp.zeros_like(st_scr)

    gate_pre = jnp.dot(d_ref[0].astype(BF16), wg_ref[...],
                       preferred_element_type=F32) + bg_ref[...]
    log_a = _log_sigmoid(gate_pre) / GLA_TAU
    row = lax.broadcasted_iota(jnp.int32, (blk, blk), 0)
    col = lax.broadcasted_iota(jnp.int32, (blk, blk), 1)
    same_chunk = (row // ch) == (col // ch)
    tri = jnp.where((col <= row) & same_chunk, 1.0, 0.0).astype(F32)
    bc = jnp.dot(tri, log_a, preferred_element_type=F32,
                 precision=lax.Precision.HIGHEST)
    e_pos = jnp.exp(bc)
    e_neg = jnp.exp(-bc)
    crow = lax.broadcasted_iota(jnp.int32, (ch, ch), 0)
    ccol = lax.broadcasted_iota(jnp.int32, (ch, ch), 1)
    causal = ccol <= crow

    for sc in range(blk // ch):
        r0, r1 = sc * ch, (sc + 1) * ch
        outs = []
        for h in range(GLA_HEADS):
            k0, k1 = h * GLA_DK, (h + 1) * GLA_DK
            q = b_ref[0, r0:r1, k0:k1].astype(F32) * (GLA_DK ** -0.5)
            k = b_ref[0, r0:r1, GLA_KW + k0:GLA_KW + k1].astype(F32)
            v = b_ref[0, r0:r1, 2 * GLA_KW + h * GLA_DV:2 * GLA_KW + (h + 1) * GLA_DV]
            rg = b_ref[0, r0:r1, 2 * GLA_KW + GLA_VW + h * GLA_DV:
                       2 * GLA_KW + GLA_VW + (h + 1) * GLA_DV].astype(F32)
            bc_h = bc[r0:r1, k0:k1]
            b_last = bc_h[ch - 1:ch, :]
            q_dec = (q * e_pos[r0:r1, k0:k1]).astype(BF16)
            k_inv = (k * e_neg[r0:r1, k0:k1]).astype(BF16)
            attn = lax.dot_general(q_dec, k_inv, (((1,), (1,)), ((), ())),
                                   preferred_element_type=F32)
            attn = jnp.where(causal, attn, 0.0)
            st = st_scr[h]
            o = (jnp.dot(attn.astype(BF16), v, preferred_element_type=F32)
                 + lax.dot_general(q_dec, st.astype(BF16), (((1,), (1,)), ((), ())),
                                   preferred_element_type=F32))
            k_w = (k * jnp.exp(b_last - bc_h)).astype(BF16)
            s_loc_t = lax.dot_general(v, k_w, (((0,), (0,)), ((), ())),
                                      preferred_element_type=F32)
            st_scr[h] = st * jnp.exp(b_last) + s_loc_t
            on = o * lax.rsqrt(jnp.mean(o * o, axis=-1, keepdims=True) + NORM_EPS)
            outs.append(rg * _sigmoid(rg) * on)
        o_ref[0, r0:r1, :] = jnp.concatenate(outs, axis=-1).astype(o_ref.dtype)


def _gla(bg, dg, wg_pad, gla_bg, l):
    bsz, seq, _ = bg.shape
    blk = _tile(seq, GLA_BLOCK)
    return pl.pallas_call(
        _gla_kernel,
        out_shape=jax.ShapeDtypeStruct((bsz, seq, GLA_VW), BF16),
        grid=(bsz, seq // blk),
        in_specs=[
            pl.BlockSpec((1, blk, GRP_B), lambda bb, c: (bb, c, 0)),
            pl.BlockSpec((1, blk, GRP_D), lambda bb, c: (bb, c, 0)),
            _layer_spec(wg_pad, l, 2),
            _layer_spec(gla_bg, l, 2),
        ],
        out_specs=pl.BlockSpec((1, blk, GLA_VW), lambda bb, c: (bb, c, 0)),
        scratch_shapes=[pltpu.VMEM((GLA_HEADS, GLA_DV, GLA_DK), F32)],
        compiler_params=_cparams(("parallel", "arbitrary")),
        name="gla",
    )(bg, dg, wg_pad, gla_bg)


def _rms(x, g_ref):
    return x * lax.rsqrt(jnp.mean(x * x, axis=-1, keepdims=True) + NORM_EPS) * g_ref[...]


def _dot_nt(a, b):
    return lax.dot_general(a, b, (((1,), (1,)), ((), ())), preferred_element_type=F32)


def _mla_proj_kernel(c_ref, d_ref, cos_ref, sin_ref, cost_ref, sint_ref, gq_ref, gkv_ref,
                     wqn_ref, wq1_ref, wq2_ref, wuk_ref, wuv_ref, qt_ref, k_ref, vt_ref):
    cq = c_ref[0, :, 0:MLA_RANK].astype(F32)
    ckv = c_ref[0, :, MLA_RANK:2 * MLA_RANK].astype(F32)
    qn = _rms(cq, gq_ref).astype(BF16)
    kvn = _rms(ckv, gkv_ref).astype(BF16)

    q_nope_t = _dot_nt(wqn_ref[...], qn)
    q1_t = _dot_nt(wq1_ref[...], qn)
    q2_t = _dot_nt(wq2_ref[...], qn)
    cos_t = jnp.concatenate([cost_ref[0]] * MLA_HEADS, axis=0)
    sin_t = jnp.concatenate([sint_ref[0]] * MLA_HEADS, axis=0)
    qr1_t = q1_t * cos_t - q2_t * sin_t
    qr2_t = q1_t * sin_t + q2_t * cos_t

    k_nope = jnp.dot(kvn, wuk_ref[...], preferred_element_type=F32)
    v_t = _dot_nt(wuv_ref[...], kvn)
    kr = d_ref[0]
    k1 = kr[:, D_KR:D_KR + MLA_HALF]
    k2 = kr[:, D_KR + MLA_HALF:D_KR + MLA_ROPE]
    cos = cos_ref[0]
    sin = sin_ref[0]
    kr1 = (k1 * cos - k2 * sin).astype(BF16)
    kr2 = (k1 * sin + k2 * cos).astype(BF16)
    for h in range(MLA_HEADS):
        n0, n1 = h * MLA_NOPE, (h + 1) * MLA_NOPE
        h0, h1 = h * MLA_HALF, (h + 1) * MLA_HALF
        q_t = jnp.concatenate([q_nope_t[n0:n1], qr1_t[h0:h1], qr2_t[h0:h1]], axis=0)
        qt_ref[0, h] = (q_t * QK_SCALE_LOG2).astype(BF16)
        k_ref[0, h] = jnp.concatenate([k_nope[:, n0:n1].astype(BF16), kr1, kr2], axis=-1)
        vt_ref[0, h] = v_t[h * MLA_DV:(h + 1) * MLA_DV].astype(BF16)


def _mla_proj(cg, dg, tables, gq, gkv, wqn_t, wq1_t, wq2_t, wuk, wuv_t, l):
    bsz, seq, _ = cg.shape
    tm = _tile(seq, 512)
    cos, sin, cos_t, sin_t = tables

    def row(w):
        return pl.BlockSpec((1, tm, w), lambda bb, i: (bb, i, 0))

    def col(w):
        return pl.BlockSpec((1, w, tm), lambda bb, i: (bb, 0, i))

    return pl.pallas_call(
        _mla_proj_kernel,
        out_shape=(jax.ShapeDtypeStruct((bsz, MLA_HEADS, MLA_QK, seq), BF16),
                   jax.ShapeDtypeStruct((bsz, MLA_HEADS, seq, MLA_QK), BF16),
                   jax.ShapeDtypeStruct((bsz, MLA_HEADS, MLA_DV, seq), BF16)),
        grid=(bsz, seq // tm),
        in_specs=[row(GRP_C), row(GRP_D), row(MLA_HALF), row(MLA_HALF),
                  col(MLA_HALF), col(MLA_HALF),
                  _layer_spec(gq, l, 2), _layer_spec(gkv, l, 2),
                  _layer_spec(wqn_t, l, 2), _layer_spec(wq1_t, l, 2), _layer_spec(wq2_t, l, 2),
                  _layer_spec(wuk, l, 2), _layer_spec(wuv_t, l, 2)],
        out_specs=(pl.BlockSpec((1, MLA_HEADS, MLA_QK, tm), lambda bb, i: (bb, 0, 0, i)),
                   pl.BlockSpec((1, MLA_HEADS, tm, MLA_QK), lambda bb, i: (bb, 0, i, 0)),
                   pl.BlockSpec((1, MLA_HEADS, MLA_DV, tm), lambda bb, i: (bb, 0, 0, i))),
        compiler_params=_cparams(("parallel", "parallel")),
        name="mla_proj",
    )(cg, dg, cos, sin, cos_t, sin_t, gq, gkv, wqn_t, wq1_t, wq2_t, wuk, wuv_t)


def _attn_kernel(qt_ref, k_ref, vt_ref, o_ref, sa_ref, sb_ref, m_ref, l_ref, acc_ref):
    tq = qt_ref.shape[3]
    i = pl.program_id(2)
    qt = qt_ref[0, 0]

    def scores(j, s_ref):
        start = pl.multiple_of(j * tq, tq)
        k = k_ref[0, 0, pl.ds(start, tq), :]
        s_ref[...] = jnp.dot(k, qt, preferred_element_type=F32)

    def update(j, s_ref, masked):
        start = pl.multiple_of(j * tq, tq)
        vt = vt_ref[0, 0, :, pl.ds(start, tq)]
        st = s_ref[...]
        if masked:
            key = lax.broadcasted_iota(jnp.int32, (tq, tq), 0)
            qry = lax.broadcasted_iota(jnp.int32, (tq, tq), 1)
            st = jnp.where(key <= qry, st, -jnp.inf)
        m = m_ref[...]
        m_new = jnp.maximum(m, jnp.max(st, axis=0, keepdims=True))
        pt = jnp.exp2(st - m_new)
        corr = jnp.exp2(m - m_new)
        l_ref[...] = corr * l_ref[...] + jnp.sum(pt, axis=0, keepdims=True)
        acc_ref[...] = corr * acc_ref[...] + jnp.dot(vt, pt.astype(BF16),
                                                      preferred_element_type=F32)
        m_ref[...] = m_new

    m_ref[...] = jnp.full(m_ref.shape, -jnp.inf, F32)
    l_ref[...] = jnp.zeros(l_ref.shape, F32)
    acc_ref[...] = jnp.zeros(acc_ref.shape, F32)
    scores(0, sa_ref)

    def body(t, carry):
        scores(2 * t + 1, sb_ref)
        update(2 * t, sa_ref, False)
        scores(2 * t + 2, sa_ref)
        update(2 * t + 1, sb_ref, False)
        return carry

    lax.fori_loop(0, i // 2, body, 0)

    @pl.when(i % 2 == 0)
    def _():
        update(i, sa_ref, True)

    @pl.when(i % 2 == 1)
    def _():
        scores(i, sb_ref)
        update(i - 1, sa_ref, False)
        update(i, sb_ref, True)

    o_ref[0] = (acc_ref[...] / l_ref[...]).T.astype(o_ref.dtype)


def _attention(qt, k, vt):
    bsz, nh, seq, _ = k.shape
    tq = _tile(seq, ATTN_BLOCK)
    return pl.pallas_call(
        _attn_kernel,
        out_shape=jax.ShapeDtypeStruct((bsz, seq, nh * MLA_DV), BF16),
        grid=(bsz, nh, seq // tq),
        in_specs=[
            pl.BlockSpec((1, 1, MLA_QK, tq), lambda bb, h, i: (bb, h, 0, i)),
            pl.BlockSpec((1, 1, seq, MLA_QK), lambda bb, h, i: (bb, h, 0, 0)),
            pl.BlockSpec((1, 1, MLA_DV, seq), lambda bb, h, i: (bb, h, 0, 0)),
        ],
        out_specs=pl.BlockSpec((1, tq, MLA_DV), lambda bb, h, i: (bb, i, h)),
        scratch_shapes=[pltpu.VMEM((tq, tq), F32), pltpu.VMEM((tq, tq), F32),
                        pltpu.VMEM((1, tq), F32), pltpu.VMEM((1, tq), F32),
                        pltpu.VMEM((MLA_DV, tq), F32)],
        compiler_params=_cparams(("parallel", "parallel", "arbitrary")),
        name="mla_attn",
    )(qt, k, vt)


def _outproj_kernel(x_ref, mod_ref, yml_ref, ygla_ref, ymla_ref, w_ref, g_ref, b_ref, o_ref,
                    *, alpha):
    y = jnp.concatenate([yml_ref[0], ygla_ref[0], ymla_ref[0]], axis=-1)
    r = jnp.dot(y, w_ref[...], preferred_element_type=F32) * (1.0 + mod_ref[0, 2:3, :])
    o_ref[0] = _residual_layer_norm(x_ref[0], r, alpha, g_ref, b_ref)


def _outproj(x, mod, y_ml, y_gla, y_mla, w_out, ln_g, ln_b, l, alpha):
    bsz, seq, d = x.shape
    tm = _tile(seq, 512)

    def row(w):
        return pl.BlockSpec((1, tm, w), lambda bb, i: (bb, i, 0))

    return pl.pallas_call(
        functools.partial(_outproj_kernel, alpha=alpha),
        out_shape=jax.ShapeDtypeStruct(x.shape, F32),
        grid=(bsz, seq // tm),
        in_specs=[
            row(d),
            _mod_spec(mod, l, 1),
            row(y_ml.shape[-1]), row(y_gla.shape[-1]), row(y_mla.shape[-1]),
            pl.BlockSpec((None,) + w_out.shape[1:], lambda bb, i: (l, 0, 0),
                         pipeline_mode=pl.Buffered(1)),
            _ln_spec(ln_g, l, 1),
            _ln_spec(ln_b, l, 1),
        ],
        out_specs=row(d),
        compiler_params=_cparams(("parallel", "parallel")),
        name="outproj",
    )(x, mod, y_ml, y_gla, y_mla, w_out, ln_g, ln_b)


def _regroup_w_in(w_in):
    sizes = (ML_W, ML_W, ML_W, ML_HEADS, ML_HEADS, ML_W,
             GLA_KW, GLA_KW, GLA_VW, GLA_RANK, GLA_VW,
             MLA_RANK, MLA_RANK, MLA_ROPE)
    offs = np.concatenate([[0], np.cumsum(sizes)])
    (ml_q, ml_k, ml_v, ml_i, ml_f, ml_o, gl_q, gl_k, gl_v, gl_lr, gl_r, c_q, c_kv, k_r) = (
        w_in[..., int(offs[n]):int(offs[n + 1])] for n in range(len(sizes)))
    used = MLA_ROPE + 2 * ML_HEADS + GLA_RANK
    pad = jnp.zeros(w_in.shape[:-1] + (GRP_D - used,), w_in.dtype)
    return jnp.concatenate([ml_q, ml_k, ml_v, ml_o, gl_q, gl_k, gl_v, gl_r, c_q, c_kv,
                            k_r, ml_i, ml_f, gl_lr, pad], axis=-1).astype(BF16)


def kernel(x, c, positions, w_ada, b_ada, ln_g, ln_b, ffn1_wi, ffn1_wo, ffn2_wi, ffn2_wo,
           w_in, ml_conv, ml_bi, ml_bf, gla_wg, gla_bg, mla_gq, mla_wuq, mla_gkv,
           mla_wuk, mla_wuv, w_out):
    depth = w_ada.shape[0]
    bsz, seq, d = x.shape
    alpha = (2.0 * depth) ** 0.25

    mod = _ada_mod(c, w_ada, b_ada).reshape(depth, bsz, 3, 3, d)
    tables = _rope_tables(positions)

    w_in_r = _regroup_w_in(w_in)
    ffn1_wi_b, ffn1_wo_b = ffn1_wi.astype(BF16), ffn1_wo.astype(BF16)
    ffn2_wi_b, ffn2_wo_b = ffn2_wi.astype(BF16), ffn2_wo.astype(BF16)
    w_out_b = w_out.astype(BF16)
    ln_g4 = ln_g.reshape(depth, 3, 1, d)
    ln_b4 = ln_b.reshape(depth, 3, 1, d)
    wuq = mla_wuq.reshape(depth, MLA_RANK, MLA_HEADS, MLA_QK)

    def feat_major(w):
        return jnp.swapaxes(w.reshape(depth, MLA_RANK, -1), 1, 2).astype(BF16)

    wqn_t = feat_major(wuq[..., :MLA_NOPE])
    wq1_t = feat_major(wuq[..., MLA_NOPE:MLA_NOPE + MLA_HALF])
    wq2_t = feat_major(wuq[..., MLA_NOPE + MLA_HALF:])
    wuk_b = mla_wuk.astype(BF16)
    wuv_t = feat_major(mla_wuv)
    gq3 = mla_gq.reshape(depth, 1, MLA_RANK)
    gkv3 = mla_gkv.reshape(depth, 1, MLA_RANK)
    gate_bias = jnp.zeros((depth, 1, GRP_D), F32)
    gate_bias = gate_bias.at[:, 0, D_MLI:D_MLI + ML_HEADS].set(ml_bi)
    gate_bias = gate_bias.at[:, 0, D_MLF:D_MLF + ML_HEADS].set(ml_bf)
    wg_pad = jnp.zeros((depth, GRP_D, GLA_KW), F32)
    wg_pad = wg_pad.at[:, D_GLR:D_GLR + GLA_RANK, :].set(gla_wg).astype(BF16)
    gla_bg3 = gla_bg.reshape(depth, 1, GLA_KW)

    for l in range(depth):
        x = _ffn(x, mod, ffn1_wi_b, ffn1_wo_b, ln_g4, ln_b4, l, 0, alpha)

        ag, bg, cg, dg = _inproj(x, mod, w_in_r, l)
        y_ml = _mlstm(ag, dg, ml_conv, gate_bias, l)
        y_gla = _gla(bg, dg, wg_pad, gla_bg3, l)
        qt, k, vt = _mla_proj(cg, dg, tables, gq3, gkv3, wqn_t, wq1_t, wq2_t, wuk_b, wuv_t, l)
        y_mla = _attention(qt, k, vt)
        x = _outproj(x, mod, y_ml, y_gla, y_mla, w_out_b, ln_g4, ln_b4, l, alpha)

        x = _ffn(x, mod, ffn2_wi_b, ffn2_wo_b, ln_g4, ln_b4, l, 2, alpha)
    return x
```

```python
import functools
import math

import jax
import jax.numpy as jnp
import numpy as np
from jax import lax
from jax.experimental import pallas as pl
from jax.experimental.pallas import tpu as pltpu

F32 = jnp.float32
BF16 = jnp.bfloat16

ML_HEADS = 4
ML_DH = 128
ML_W = ML_HEADS * ML_DH
ML_CONV = 4
GLA_HEADS = 4
GLA_DK = 64
GLA_DV = 128
GLA_KW = GLA_HEADS * GLA_DK
GLA_VW = GLA_HEADS * GLA_DV
GLA_RANK = 16
GLA_TAU = 16.0
GLA_CHUNK = 64
MLA_HEADS = 8
MLA_NOPE = 128
MLA_ROPE = 64
MLA_HALF = MLA_ROPE // 2
MLA_DV = 128
MLA_QK = MLA_NOPE + MLA_ROPE
MLA_RANK = 512
ROPE_THETA = 10000.0
NORM_EPS = 1e-5
FFN_RES_WEIGHT = 0.5
QK_SCALE_LOG2 = math.log2(math.e) * MLA_QK ** -0.5

GRP_A = 4 * ML_W
GRP_B = 2 * GLA_KW + 2 * GLA_VW
GRP_C = 2 * MLA_RANK
GRP_D = 128
D_KR = 0
D_MLI = MLA_ROPE
D_MLF = D_MLI + ML_HEADS
D_GLR = D_MLF + ML_HEADS

VMEM_LIMIT_V7X = 58 * 1024 * 1024

FFN_ROWS = 1024
FFN_COLS = 512
ML_BLOCK = 256
GLA_BLOCK = 256
ATTN_BLOCK = 512


def _cparams(sem):
    return pltpu.CompilerParams(dimension_semantics=sem, vmem_limit_bytes=VMEM_LIMIT_V7X)


def _tile(n, pref):
    t = min(n, pref)
    while n % t:
        t //= 2
    return t


def _layer_spec(arr, l, nargs):
    shape = (None,) + arr.shape[1:]
    zeros = (0,) * (arr.ndim - 1)
    return pl.BlockSpec(shape, lambda *_: (l,) + zeros)


def _mod_spec(mod, l, k):
    d = mod.shape[-1]
    return pl.BlockSpec((None, 1, None, 3, d), lambda bb, *_: (l, bb, k, 0, 0))


def _ln_spec(ln, l, k):
    return pl.BlockSpec((None, None, 1, ln.shape[-1]), lambda *_: (l, k, 0, 0))


def _sigmoid(x):
    return jax.nn.sigmoid(x)


def _log_sigmoid(x):
    return jnp.minimum(x, 0.0) - jnp.log1p(jnp.exp(-jnp.abs(x)))


def _modulate(x, mod_ref):
    return x * (1.0 + mod_ref[0, 1:2, :]) + mod_ref[0, 0:1, :]


def _residual_layer_norm(x, r, alpha, g_ref, b_ref):
    z = alpha * x + r
    mu = jnp.mean(z, axis=-1, keepdims=True)
    zc = z - mu
    var = jnp.mean(zc * zc, axis=-1, keepdims=True)
    return zc * lax.rsqrt(var + NORM_EPS) * g_ref[...] + b_ref[...]


def _ada_kernel(c_ref, w_ref, b_ref, o_ref):
    c = c_ref[...]
    ca = (c * _sigmoid(c)).astype(BF16)
    o_ref[0] = jnp.dot(ca, w_ref[0].astype(BF16), preferred_element_type=F32) + b_ref[0]


def _ada_mod(c, w_ada, b_ada):
    depth, d, n = w_ada.shape
    bsz = c.shape[0]
    tn = _tile(n, 1024)
    return pl.pallas_call(
        _ada_kernel,
        out_shape=jax.ShapeDtypeStruct((depth, bsz, n), F32),
        grid=(depth, n // tn),
        in_specs=[
            pl.BlockSpec((bsz, d), lambda l, j: (0, 0)),
            pl.BlockSpec((1, d, tn), lambda l, j: (l, 0, j)),
            pl.BlockSpec((1, 1, tn), lambda l, j: (l, 0, j)),
        ],
        out_specs=pl.BlockSpec((1, bsz, tn), lambda l, j: (l, 0, j)),
        compiler_params=_cparams(("parallel", "parallel")),
        name="ada_mod",
    )(c, w_ada, b_ada.reshape(depth, 1, n))


def _rope_kernel(pcol_ref, prow_ref, frow_ref, fcol_ref, cos_ref, sin_ref, cost_ref, sint_ref):
    ang = pcol_ref[0].astype(F32) * frow_ref[...]
    cos_ref[0] = jnp.cos(ang)
    sin_ref[0] = jnp.sin(ang)
    ang_t = fcol_ref[...] * prow_ref[0].astype(F32)
    cost_ref[0] = jnp.cos(ang_t)
    sint_ref[0] = jnp.sin(ang_t)


def _rope_tables(positions):
    bsz, seq = positions.shape
    ts = _tile(seq, 1024)
    inv_freq = (ROPE_THETA ** (-np.arange(MLA_HALF, dtype=np.float32) / MLA_HALF)).astype(np.float32)
    nat = jax.ShapeDtypeStruct((bsz, seq, MLA_HALF), F32)
    tra = jax.ShapeDtypeStruct((bsz, MLA_HALF, seq), F32)
    return pl.pallas_call(
        _rope_kernel,
        out_shape=(nat, nat, tra, tra),
        grid=(bsz, seq // ts),
        in_specs=[
            pl.BlockSpec((1, ts, 1), lambda b, i: (b, i, 0)),
            pl.BlockSpec((1, 1, ts), lambda b, i: (b, 0, i)),
            pl.BlockSpec((1, MLA_HALF), lambda b, i: (0, 0)),
            pl.BlockSpec((MLA_HALF, 1), lambda b, i: (0, 0)),
        ],
        out_specs=(pl.BlockSpec((1, ts, MLA_HALF), lambda b, i: (b, i, 0)),
                   pl.BlockSpec((1, ts, MLA_HALF), lambda b, i: (b, i, 0)),
                   pl.BlockSpec((1, MLA_HALF, ts), lambda b, i: (b, 0, i)),
                   pl.BlockSpec((1, MLA_HALF, ts), lambda b, i: (b, 0, i))),
        compiler_params=_cparams(("parallel", "parallel")),
        name="rope_tables",
    )(positions.reshape(bsz, seq, 1), positions.reshape(bsz, 1, seq),
      jnp.asarray(inv_freq[None, :]), jnp.asarray(inv_freq[:, None]))


def _ffn_kernel(x_ref, mod_ref, wig_ref, wiu_ref, wo_ref, g_ref, b_ref, o_ref, *, alpha):
    j = pl.program_id(2)

    @pl.when(j == 0)
    def _():
        o_ref[0] = jnp.zeros(o_ref.shape[1:], F32)

    half = x_ref.shape[1] // 2
    for r in range(2):
        rows = pl.ds(r * half, half)
        u = _modulate(x_ref[0, rows, :], mod_ref).astype(BF16)
        g = jnp.dot(u, wig_ref[...], preferred_element_type=F32)
        up = jnp.dot(u, wiu_ref[...], preferred_element_type=F32)
        a = (g * _sigmoid(g) * up).astype(BF16)
        o_ref[0, rows, :] += jnp.dot(a, wo_ref[...], preferred_element_type=F32)

    @pl.when(j == pl.num_programs(2) - 1)
    def _():
        gate = FFN_RES_WEIGHT * (1.0 + mod_ref[0, 2:3, :])
        quarter = x_ref.shape[1] // 4
        for c in range(4):
            rows = pl.ds(c * quarter, quarter)
            o_ref[0, rows, :] = _residual_layer_norm(x_ref[0, rows, :], o_ref[0, rows, :] * gate,
                                                     alpha, g_ref, b_ref)


def _ffn(x, mod, wi, wo, ln_g, ln_b, l, k, alpha):
    bsz, seq, d = x.shape
    f = wo.shape[1]
    tm = _tile(seq, FFN_ROWS)
    tf = _tile(f, FFN_COLS)
    nf = f // tf
    return pl.pallas_call(
        functools.partial(_ffn_kernel, alpha=alpha),
        out_shape=jax.ShapeDtypeStruct(x.shape, F32),
        grid=(bsz, seq // tm, nf),
        in_specs=[
            pl.BlockSpec((1, tm, d), lambda bb, i, j: (bb, i, 0)),
            _mod_spec(mod, l, k),
            pl.BlockSpec((None, d, tf), lambda bb, i, j: (l, 0, j)),
            pl.BlockSpec((None, d, tf), lambda bb, i, j: (l, 0, j + nf)),
            pl.BlockSpec((None, tf, d), lambda bb, i, j: (l, j, 0)),
            _ln_spec(ln_g, l, k),
            _ln_spec(ln_b, l, k),
        ],
        out_specs=pl.BlockSpec((1, tm, d), lambda bb, i, j: (bb, i, 0)),
        compiler_params=_cparams(("parallel", "parallel", "arbitrary")),
        name="ffn",
    )(x, mod, wi, wi, wo, ln_g, ln_b)


def _inproj_kernel(x_ref, mod_ref, w_ref, a_ref, b_ref, c_ref, d_ref):
    u = _modulate(x_ref[0], mod_ref).astype(BF16)
    o0, o1, o2, o3 = 0, GRP_A, GRP_A + GRP_B, GRP_A + GRP_B + GRP_C
    a_ref[0] = jnp.dot(u, w_ref[:, o0:o1], preferred_element_type=F32).astype(BF16)
    b_ref[0] = jnp.dot(u, w_ref[:, o1:o2], preferred_element_type=F32).astype(BF16)
    c_ref[0] = jnp.dot(u, w_ref[:, o2:o3], preferred_element_type=F32).astype(BF16)
    d_ref[0] = jnp.dot(u, w_ref[:, o3:o3 + GRP_D], preferred_element_type=F32)


def _inproj(x, mod, w_r, l):
    bsz, seq, d = x.shape
    tm = _tile(seq, 512)
    n = w_r.shape[-1]

    def row(w):
        return pl.BlockSpec((1, tm, w), lambda bb, i: (bb, i, 0))

    return pl.pallas_call(
        _inproj_kernel,
        out_shape=(jax.ShapeDtypeStruct((bsz, seq, GRP_A), BF16),
                   jax.ShapeDtypeStruct((bsz, seq, GRP_B), BF16),
                   jax.ShapeDtypeStruct((bsz, seq, GRP_C), BF16),
                   jax.ShapeDtypeStruct((bsz, seq, GRP_D), F32)),
        grid=(bsz, seq // tm),
        in_specs=[
            row(d),
            _mod_spec(mod, l, 1),
            pl.BlockSpec((None, d, n), lambda bb, i: (l, 0, 0), pipeline_mode=pl.Buffered(1)),
        ],
        out_specs=(row(GRP_A), row(GRP_B), row(GRP_C), row(GRP_D)),
        compiler_params=_cparams(("parallel", "parallel")),
        name="inproj",
    )(x, mod, w_r)


def _mlstm_kernel(a_ref, d_ref, conv_ref, bias_ref, o_ref, xbuf, c_scr, n_scr, m_scr):
    blk = a_ref.shape[1]
    pad = 8

    @pl.when(pl.program_id(1) == 0)
    def _():
        xbuf[0:pad, :] = jnp.zeros((pad, 2 * ML_W), F32)
        c_scr[...] = jnp.zeros_like(c_scr)
        n_scr[...] = jnp.zeros_like(n_scr)
        m_scr[...] = jnp.zeros_like(m_scr)

    xbuf[pad:pad + blk, :] = a_ref[0, :, 0:2 * ML_W].astype(F32)
    qk = conv_ref[ML_CONV - 1:ML_CONV, :] * xbuf[pad:pad + blk, :]
    for tap in range(ML_CONV - 1):
        sh = ML_CONV - 1 - tap
        qk = qk + conv_ref[tap:tap + 1, :] * xbuf[pad - sh:pad - sh + blk, :]
    qk = qk * _sigmoid(qk)
    xbuf[0:pad, :] = xbuf[blk:blk + pad, :]

    pre = d_ref[0] + bias_ref[...]
    row = lax.broadcasted_iota(jnp.int32, (blk, blk), 0)
    col = lax.broadcasted_iota(jnp.int32, (blk, blk), 1)
    causal = col <= row
    eye = col == row
    tri = jnp.where(causal, 1.0, 0.0).astype(F32)
    bcum = jnp.dot(tri, _log_sigmoid(pre), preferred_element_type=F32,
                   precision=lax.Precision.HIGHEST)

    outs = []
    for h in range(ML_HEADS):
        b_col = bcum[:, D_MLF + h:D_MLF + h + 1]
        r_col = pre[:, D_MLI + h:D_MLI + h + 1] - b_col
        r_row = jnp.sum(jnp.where(eye, r_col, 0.0), axis=0, keepdims=True)
        b_last = b_col[blk - 1:blk, :]
        m_prev = m_scr[h, 0:1, 0:1]
        n_prev = n_scr[h]
        c_prev = c_scr[h]

        q = qk[:, h * ML_DH:(h + 1) * ML_DH]
        k = qk[:, ML_W + h * ML_DH:ML_W + (h + 1) * ML_DH] * (ML_DH ** -0.5)
        v = a_ref[0, :, 2 * ML_W + h * ML_DH:2 * ML_W + (h + 1) * ML_DH]
        og = a_ref[0, :, 3 * ML_W + h * ML_DH:3 * ML_W + (h + 1) * ML_DH].astype(F32)
        qb = q.astype(BF16)

        dmat = jnp.where(causal, b_col + r_row, -jnp.inf)
        m_inter = b_col + m_prev
        m_t = jnp.maximum(m_inter, jnp.max(dmat, axis=1, keepdims=True))
        s = lax.dot_general(qb, k.astype(BF16), (((1,), (1,)), ((), ())),
                            preferred_element_type=F32) * jnp.exp(dmat - m_t)
        inter = jnp.exp(m_inter - m_t)
        num = (jnp.dot(s.astype(BF16), v, preferred_element_type=F32)
               + inter * jnp.dot(qb, c_prev.astype(BF16), preferred_element_type=F32))
        den = (jnp.sum(s, axis=1, keepdims=True)
               + inter * jnp.sum(q * n_prev, axis=1, keepdims=True))
        hh = num / jnp.maximum(jnp.abs(den), jnp.exp(-m_t))
        mu = jnp.mean(hh, axis=-1, keepdims=True)
        hc = hh - mu
        var = jnp.mean(hc * hc, axis=-1, keepdims=True)
        outs.append(_sigmoid(og) * (hc * lax.rsqrt(var + NORM_EPS)))

        a_col = b_last + r_col
        a_max = jnp.max(a_col, axis=0, keepdims=True)
        kw = k * jnp.exp(a_col - a_max)
        c_loc = lax.dot_general(kw.astype(BF16), v, (((0,), (0,)), ((), ())),
                                preferred_element_type=F32)
        n_loc = jnp.sum(kw, axis=0, keepdims=True)
        m_new = jnp.maximum(b_last + m_prev, a_max)
        dec = jnp.exp(b_last + m_prev - m_new)
        inj = jnp.exp(a_max - m_new)
        c_scr[h] = dec * c_prev + inj * c_loc
        n_scr[h] = dec * n_prev + inj * n_loc
        m_scr[h] = jnp.broadcast_to(m_new, m_scr.shape[1:])

    o_ref[0] = jnp.concatenate(outs, axis=-1).astype(o_ref.dtype)


def _mlstm(a, dg, conv, gate_bias, l):
    bsz, seq, _ = a.shape
    blk = _tile(seq, ML_BLOCK)
    return pl.pallas_call(
        _mlstm_kernel,
        out_shape=jax.ShapeDtypeStruct((bsz, seq, ML_W), BF16),
        grid=(bsz, seq // blk),
        in_specs=[
            pl.BlockSpec((1, blk, GRP_A), lambda bb, c: (bb, c, 0)),
            pl.BlockSpec((1, blk, GRP_D), lambda bb, c: (bb, c, 0)),
            _layer_spec(conv, l, 2),
            _layer_spec(gate_bias, l, 2),
        ],
        out_specs=pl.BlockSpec((1, blk, ML_W), lambda bb, c: (bb, c, 0)),
        scratch_shapes=[
            pltpu.VMEM((blk + 8, 2 * ML_W), F32),
            pltpu.VMEM((ML_HEADS, ML_DH, ML_DH), F32),
            pltpu.VMEM((ML_HEADS, 1, ML_DH), F32),
            pltpu.VMEM((ML_HEADS, 1, ML_DH), F32),
        ],
        compiler_params=_cparams(("parallel", "arbitrary")),
        name="mlstm",
    )(a, dg, conv, gate_bias)


def _gla_kernel(b_ref, d_ref, wg_ref, bg_ref, o_ref, st_scr):
    blk = b_ref.shape[1]
    ch = GLA_CHUNK

    @pl.when(pl.program_id(1) == 0)
    def _():
        st_scr[...] = jnp.zeros_like(st_scr)

    gate_pre = jnp.dot(d_ref[0].astype(BF16), wg_ref[...],
                       preferred_element_type=F32) + bg_ref[...]
    log_a = _log_sigmoid(gate_pre) / GLA_TAU
    row = lax.broadcasted_iota(jnp.int32, (blk, blk), 0)
    col = lax.broadcasted_iota(jnp.int32, (blk, blk), 1)
    same_chunk = (row // ch) == (col // ch)
    tri = jnp.where((col <= row) & same_chunk, 1.0, 0.0).astype(F32)
    bc = jnp.dot(tri, log_a, preferred_element_type=F32,
                 precision=lax.Precision.HIGHEST)
    e_pos = jnp.exp(bc)
    e_neg = jnp.exp(-bc)
    crow = lax.broadcasted_iota(jnp.int32, (ch, ch), 0)
    ccol = lax.broadcasted_iota(jnp.int32, (ch, ch), 1)
    causal = ccol <= crow

    for sc in range(blk // ch):
        r0, r1 = sc * ch, (sc + 1) * ch
        outs = []
        for h in range(GLA_HEADS):
            k0, k1 = h * GLA_DK, (h + 1) * GLA_DK
            q = b_ref[0, r0:r1, k0:k1].astype(F32) * (GLA_DK ** -0.5)
            k = b_ref[0, r0:r1, GLA_KW + k0:GLA_KW + k1].astype(F32)
            v = b_ref[0, r0:r1, 2 * GLA_KW + h * GLA_DV:2 * GLA_KW + (h + 1) * GLA_DV]
            rg = b_ref[0, r0:r1, 2 * GLA_KW + GLA_VW + h * GLA_DV:
                       2 * GLA_KW + GLA_VW + (h + 1) * GLA_DV].astype(F32)
            bc_h = bc[r0:r1, k0:k1]
            b_last = bc_h[ch - 1:ch, :]
            q_dec = (q * e_pos[r0:r1, k0:k1]).astype(BF16)
            k_inv = (k * e_neg[r0:r1, k0:k1]).astype(BF16)
            attn = lax.dot_general(q_dec, k_inv, (((1,), (1,)), ((), ())),
                                   preferred_element_type=F32)
            attn = jnp.where(causal, attn, 0.0)
            st = st_scr[h]
            o = (jnp.dot(attn.astype(BF16), v, preferred_element_type=F32)
                 + lax.dot_general(q_dec, st.astype(BF16), (((1,), (1,)), ((), ())),
                                   preferred_element_type=F32))
            k_w = (k * jnp.exp(b_last - bc_h)).astype(BF16)
            s_loc_t = lax.dot_general(v, k_w, (((0,), (0,)), ((), ())),
                                      preferred_element_type=F32)
            st_scr[h] = st * jnp.exp(b_last) + s_loc_t
            on = o * lax.rsqrt(jnp.mean(o * o, axis=-1, keepdims=True) + NORM_EPS)
            outs.append(rg * _sigmoid(rg) * on)
        o_ref[0, r0:r1, :] = jnp.concatenate(outs, axis=-1).astype(o_ref.dtype)


def _gla(bg, dg, wg_pad, gla_bg, l):
    bsz, seq, _ = bg.shape
    blk = _tile(seq, GLA_BLOCK)
    return pl.pallas_call(
        _gla_kernel,
        out_shape=jax.ShapeDtypeStruct((bsz, seq, GLA_VW), BF16),
        grid=(bsz, seq // blk),
        in_specs=[
            pl.BlockSpec((1, blk, GRP_B), lambda bb, c: (bb, c, 0)),
            pl.BlockSpec((1, blk, GRP_D), lambda bb, c: (bb, c, 0)),
            _layer_spec(wg_pad, l, 2),
            _layer_spec(gla_bg, l, 2),
        ],
        out_specs=pl.BlockSpec((1, blk, GLA_VW), lambda bb, c: (bb, c, 0)),
        scratch_shapes=[pltpu.VMEM((GLA_HEADS, GLA_DV, GLA_DK), F32)],
        compiler_params=_cparams(("parallel", "arbitrary")),
        name="gla",
    )(bg, dg, wg_pad, gla_bg)


def _rms(x, g_ref):
    return x * lax.rsqrt(jnp.mean(x * x, axis=-1, keepdims=True) + NORM_EPS) * g_ref[...]


def _dot_nt(a, b):
    return lax.dot_general(a, b, (((1,), (1,)), ((), ())), preferred_element_type=F32)


def _mla_proj_kernel(c_ref, d_ref, cos_ref, sin_ref, cost_ref, sint_ref, gq_ref, gkv_ref,
                     wqn_ref, wq1_ref, wq2_ref, wuk_ref, wuv_ref, qt_ref, k_ref, vt_ref):
    cq = c_ref[0, :, 0:MLA_RANK].astype(F32)
    ckv = c_ref[0, :, MLA_RANK:2 * MLA_RANK].astype(F32)
    qn = _rms(cq, gq_ref).astype(BF16)
    kvn = _rms(ckv, gkv_ref).astype(BF16)

    q_nope_t = _dot_nt(wqn_ref[...], qn)
    q1_t = _dot_nt(wq1_ref[...], qn)
    q2_t = _dot_nt(wq2_ref[...], qn)
    cos_t = jnp.concatenate([cost_ref[0]] * MLA_HEADS, axis=0)
    sin_t = jnp.concatenate([sint_ref[0]] * MLA_HEADS, axis=0)
    qr1_t = q1_t * cos_t - q2_t * sin_t
    qr2_t = q1_t * sin_t + q2_t * cos_t

    k_nope = jnp.dot(kvn, wuk_ref[...], preferred_element_type=F32)
    v_t = _dot_nt(wuv_ref[...], kvn)
    kr = d_ref[0]
    k1 = kr[:, D_KR:D_KR + MLA_HALF]
    k2 = kr[:, D_KR + MLA_HALF:D_KR + MLA_ROPE]
    cos = cos_ref[0]
    sin = sin_ref[0]
    kr1 = (k1 * cos - k2 * sin).astype(BF16)
    kr2 = (k1 * sin + k2 * cos).astype(BF16)
    for h in range(MLA_HEADS):
        n0, n1 = h * MLA_NOPE, (h + 1) * MLA_NOPE
        h0, h1 = h * MLA_HALF, (h + 1) * MLA_HALF
        q_t = jnp.concatenate([q_nope_t[n0:n1], qr1_t[h0:h1], qr2_t[h0:h1]], axis=0)
        qt_ref[0, h] = (q_t * QK_SCALE_LOG2).astype(BF16)
        k_ref[0, h] = jnp.concatenate([k_nope[:, n0:n1].astype(BF16), kr1, kr2], axis=-1)
        vt_ref[0, h] = v_t[h * MLA_DV:(h + 1) * MLA_DV].astype(BF16)


def _mla_proj(cg, dg, tables, gq, gkv, wqn_t, wq1_t, wq2_t, wuk, wuv_t, l):
    bsz, seq, _ = cg.shape
    tm = _tile(seq, 512)
    cos, sin, cos_t, sin_t = tables

    def row(w):
        return pl.BlockSpec((1, tm, w), lambda bb, i: (bb, i, 0))

    def col(w):
        return pl.BlockSpec((1, w, tm), lambda bb, i: (bb, 0, i))

    return pl.pallas_call(
        _mla_proj_kernel,
        out_shape=(jax.ShapeDtypeStruct((bsz, MLA_HEADS, MLA_QK, seq), BF16),
                   jax.ShapeDtypeStruct((bsz, MLA_HEADS, seq, MLA_QK), BF16),
                   jax.ShapeDtypeStruct((bsz, MLA_HEADS, MLA_DV, seq), BF16)),
        grid=(bsz, seq // tm),
        in_specs=[row(GRP_C), row(GRP_D), row(MLA_HALF), row(MLA_HALF),
                  col(MLA_HALF), col(MLA_HALF),
                  _layer_spec(gq, l, 2), _layer_spec(gkv, l, 2),
                  _layer_spec(wqn_t, l, 2), _layer_spec(wq1_t, l, 2), _layer_spec(wq2_t, l, 2),
                  _layer_spec(wuk, l, 2), _layer_spec(wuv_t, l, 2)],
        out_specs=(pl.BlockSpec((1, MLA_HEADS, MLA_QK, tm), lambda bb, i: (bb, 0, 0, i)),
                   pl.BlockSpec((1, MLA_HEADS, tm, MLA_QK), lambda bb, i: (bb, 0, i, 0)),
                   pl.BlockSpec((1, MLA_HEADS, MLA_DV, tm), lambda bb, i: (bb, 0, 0, i))),
        compiler_params=_cparams(("parallel", "parallel")),
        name="mla_proj",
    )(cg, dg, cos, sin, cos_t, sin_t, gq, gkv, wqn_t, wq1_t, wq2_t, wuk, wuv_t)


def _attn_kernel(qt_ref, k_ref, vt_ref, o_ref, s_ref, m_ref, l_ref, acc_ref, *, tq):
    seq = k_ref.shape[2]
    nq = seq // tq
    key = lax.broadcasted_iota(jnp.int32, (tq, tq), 0)
    qry = lax.broadcasted_iota(jnp.int32, (tq, tq), 1)
    tri = key <= qry
    tasks = [(i, j) for i in range(nq) for j in range(i + 1)]

    def scores(t):
        i, j = tasks[t]
        qt = qt_ref[0, 0, :, i * tq:(i + 1) * tq]
        k = k_ref[0, 0, j * tq:(j + 1) * tq, :]
        s_ref[t % 2] = jnp.dot(k, qt, preferred_element_type=F32)

    def update(t):
        i, j = tasks[t]
        slot = i % 2
        vt = vt_ref[0, 0, :, j * tq:(j + 1) * tq]
        st = s_ref[t % 2]
        if j == i:
            st = jnp.where(tri, st, -jnp.inf)
        if j == 0:
            m_new = jnp.max(st, axis=0, keepdims=True)
            pt = jnp.exp2(st - m_new)
            l_new = jnp.sum(pt, axis=0, keepdims=True)
            acc_new = jnp.dot(vt, pt.astype(BF16), preferred_element_type=F32)
        else:
            m = m_ref[slot]
            m_new = jnp.maximum(m, jnp.max(st, axis=0, keepdims=True))
            pt = jnp.exp2(st - m_new)
            corr = jnp.exp2(m - m_new)
            l_new = corr * l_ref[slot] + jnp.sum(pt, axis=0, keepdims=True)
            acc_new = corr * acc_ref[slot] + jnp.dot(vt, pt.astype(BF16),
                                                     preferred_element_type=F32)
        if j == i:
            o_ref[0, i * tq:(i + 1) * tq, :] = (acc_new / l_new).T.astype(o_ref.dtype)
        else:
            m_ref[slot] = m_new
            l_ref[slot] = l_new
            acc_ref[slot] = acc_new

    scores(0)
    for t in range(len(tasks)):
        if t + 1 < len(tasks):
            scores(t + 1)
        update(t)


def _attention(qt, k, vt):
    bsz, nh, seq, _ = k.shape
    tq = _tile(seq, ATTN_BLOCK)
    return pl.pallas_call(
        functools.partial(_attn_kernel, tq=tq),
        out_shape=jax.ShapeDtypeStruct((bsz, seq, nh * MLA_DV), BF16),
        grid=(bsz, nh),
        in_specs=[
            pl.BlockSpec((1, 1, MLA_QK, seq), lambda bb, h: (bb, h, 0, 0)),
            pl.BlockSpec((1, 1, seq, MLA_QK), lambda bb, h: (bb, h, 0, 0)),
            pl.BlockSpec((1, 1, MLA_DV, seq), lambda bb, h: (bb, h, 0, 0)),
        ],
        out_specs=pl.BlockSpec((1, seq, MLA_DV), lambda bb, h: (bb, 0, h)),
        scratch_shapes=[pltpu.VMEM((2, tq, tq), F32), pltpu.VMEM((2, 1, tq), F32),
                        pltpu.VMEM((2, 1, tq), F32), pltpu.VMEM((2, MLA_DV, tq), F32)],
        compiler_params=_cparams(("parallel", "parallel")),
        name="mla_attn",
    )(qt, k, vt)


def _outproj_kernel(x_ref, mod_ref, yml_ref, ygla_ref, ymla_ref, w_ref, g_ref, b_ref, o_ref,
                    *, alpha):
    y = jnp.concatenate([yml_ref[0], ygla_ref[0], ymla_ref[0]], axis=-1)
    r = jnp.dot(y, w_ref[...], preferred_element_type=F32) * (1.0 + mod_ref[0, 2:3, :])
    o_ref[0] = _residual_layer_norm(x_ref[0], r, alpha, g_ref, b_ref)


def _outproj(x, mod, y_ml, y_gla, y_mla, w_out, ln_g, ln_b, l, alpha):
    bsz, seq, d = x.shape
    tm = _tile(seq, 512)

    def row(w):
        return pl.BlockSpec((1, tm, w), lambda bb, i: (bb, i, 0))

    return pl.pallas_call(
        functools.partial(_outproj_kernel, alpha=alpha),
        out_shape=jax.ShapeDtypeStruct(x.shape, F32),
        grid=(bsz, seq // tm),
        in_specs=[
            row(d),
            _mod_spec(mod, l, 1),
            row(y_ml.shape[-1]), row(y_gla.shape[-1]), row(y_mla.shape[-1]),
            pl.BlockSpec((None,) + w_out.shape[1:], lambda bb, i: (l, 0, 0),
                         pipeline_mode=pl.Buffered(1)),
            _ln_spec(ln_g, l, 1),
            _ln_spec(ln_b, l, 1),
        ],
        out_specs=row(d),
        compiler_params=_cparams(("parallel", "parallel")),
        name="outproj",
    )(x, mod, y_ml, y_gla, y_mla, w_out, ln_g, ln_b)


def _regroup_w_in(w_in):
    sizes = (ML_W, ML_W, ML_W, ML_HEADS, ML_HEADS, ML_W,
             GLA_KW, GLA_KW, GLA_VW, GLA_RANK, GLA_VW,
             MLA_RANK, MLA_RANK, MLA_ROPE)
    offs = np.concatenate([[0], np.cumsum(sizes)])
    (ml_q, ml_k, ml_v, ml_i, ml_f, ml_o, gl_q, gl_k, gl_v, gl_lr, gl_r, c_q, c_kv, k_r) = (
        w_in[..., int(offs[n]):int(offs[n + 1])] for n in range(len(sizes)))
    used = MLA_ROPE + 2 * ML_HEADS + GLA_RANK
    pad = jnp.zeros(w_in.shape[:-1] + (GRP_D - used,), w_in.dtype)
    return jnp.concatenate([ml_q, ml_k, ml_v, ml_o, gl_q, gl_k, gl_v, gl_r, c_q, c_kv,
                            k_r, ml_i, ml_f, gl_lr, pad], axis=-1).astype(BF16)


def kernel(x, c, positions, w_ada, b_ada, ln_g, ln_b, ffn1_wi, ffn1_wo, ffn2_wi, ffn2_wo,
           w_in, ml_conv, ml_bi, ml_bf, gla_wg, gla_bg, mla_gq, mla_wuq, mla_gkv,
           mla_wuk, mla_wuv, w_out):
    depth = w_ada.shape[0]
    bsz, seq, d = x.shape
    alpha = (2.0 * depth) ** 0.25

    mod = _ada_mod(c, w_ada, b_ada).reshape(depth, bsz, 3, 3, d)
    tables = _rope_tables(positions)

    w_in_r = _regroup_w_in(w_in)
    ffn1_wi_b, ffn1_wo_b = ffn1_wi.astype(BF16), ffn1_wo.astype(BF16)
    ffn2_wi_b, ffn2_wo_b = ffn2_wi.astype(BF16), ffn2_wo.astype(BF16)
    w_out_b = w_out.astype(BF16)
    ln_g4 = ln_g.reshape(depth, 3, 1, d)
    ln_b4 = ln_b.reshape(depth, 3, 1, d)
    wuq = mla_wuq.reshape(depth, MLA_RANK, MLA_HEADS, MLA_QK)

    def feat_major(w):
        return jnp.swapaxes(w.reshape(depth, MLA_RANK, -1), 1, 2).astype(BF16)

    wqn_t = feat_major(wuq[..., :MLA_NOPE])
    wq1_t = feat_major(wuq[..., MLA_NOPE:MLA_NOPE + MLA_HALF])
    wq2_t = feat_major(wuq[..., MLA_NOPE + MLA_HALF:])
    wuk_b = mla_wuk.astype(BF16)
    wuv_t = feat_major(mla_wuv)
    gq3 = mla_gq.reshape(depth, 1, MLA_RANK)
    gkv3 = mla_gkv.reshape(depth, 1, MLA_RANK)
    gate_bias = jnp.zeros((depth, 1, GRP_D), F32)
    gate_bias = gate_bias.at[:, 0, D_MLI:D_MLI + ML_HEADS].set(ml_bi)
    gate_bias = gate_bias.at[:, 0, D_MLF:D_MLF + ML_HEADS].set(ml_bf)
    wg_pad = jnp.zeros((depth, GRP_D, GLA_KW), F32)
    wg_pad = wg_pad.at[:, D_GLR:D_GLR + GLA_RANK, :].set(gla_wg).astype(BF16)
    gla_bg3 = gla_bg.reshape(depth, 1, GLA_KW)

    for l in range(depth):
        x = _ffn(x, mod, ffn1_wi_b, ffn1_wo_b, ln_g4, ln_b4, l, 0, alpha)

        ag, bg, cg, dg = _inproj(x, mod, w_in_r, l)
        y_ml = _mlstm(ag, dg, ml_conv, gate_bias, l)
        y_gla = _gla(bg, dg, wg_pad, gla_bg3, l)
        qt, k, vt = _mla_proj(cg, dg, tables, gq3, gkv3, wqn_t, wq1_t, wq2_t, wuk_b, wuv_t, l)
        y_mla = _attention(qt, k, vt)
        x = _outproj(x, mod, y_ml, y_gla, y_mla, w_out_b, ln_g4, ln_b4, l, alpha)

        x = _ffn(x, mod, ffn2_wi_b, ffn2_wo_b, ln_g4, ln_b4, l, 2, alpha)
    return x
```

```python
import functools
import math

import jax
import jax.numpy as jnp
import numpy as np
from jax import lax
from jax.experimental import pallas as pl
from jax.experimental.pallas import tpu as pltpu

F32 = jnp.float32
BF16 = jnp.bfloat16

ML_HEADS = 4
ML_DH = 128
ML_W = ML_HEADS * ML_DH
ML_CONV = 4
GLA_HEADS = 4
GLA_DK = 64
GLA_DV = 128
GLA_KW = GLA_HEADS * GLA_DK
GLA_VW = GLA_HEADS * GLA_DV
GLA_RANK = 16
GLA_TAU = 16.0
GLA_CHUNK = 64
MLA_HEADS = 8
MLA_NOPE = 128
MLA_ROPE = 64
MLA_HALF = MLA_ROPE // 2
MLA_DV = 128
MLA_QK = MLA_NOPE + MLA_ROPE
MLA_RANK = 512
ROPE_THETA = 10000.0
NORM_EPS = 1e-5
FFN_RES_WEIGHT = 0.5
QK_SCALE_LOG2 = math.log2(math.e) * MLA_QK ** -0.5

GRP_A = 4 * ML_W
GRP_B = 2 * GLA_KW + 2 * GLA_VW
GRP_C = 2 * MLA_RANK
GRP_D = 128
D_KR = 0
D_MLI = MLA_ROPE
D_MLF = D_MLI + ML_HEADS
D_GLR = D_MLF + ML_HEADS

VMEM_LIMIT_V7X = 58 * 1024 * 1024

FFN_ROWS = 1024
FFN_COLS = 512
REC_BLOCK = 256
ATTN_BLOCK = 512


def _cparams(sem):
    return pltpu.CompilerParams(dimension_semantics=sem, vmem_limit_bytes=VMEM_LIMIT_V7X)


def _tile(n, pref):
    t = min(n, pref)
    while n % t:
        t //= 2
    return t


def _layer_spec(arr, l, nargs):
    shape = (None,) + arr.shape[1:]
    zeros = (0,) * (arr.ndim - 1)
    return pl.BlockSpec(shape, lambda *_: (l,) + zeros)


def _mod_spec(mod, l, k):
    d = mod.shape[-1]
    return pl.BlockSpec((None, 1, None, 3, d), lambda bb, *_: (l, bb, k, 0, 0))


def _ln_spec(ln, l, k):
    return pl.BlockSpec((None, None, 1, ln.shape[-1]), lambda *_: (l, k, 0, 0))


def _sigmoid(x):
    return jax.nn.sigmoid(x)


def _log_sigmoid(x):
    return jnp.minimum(x, 0.0) - jnp.log1p(jnp.exp(-jnp.abs(x)))


def _modulate(x, mod_ref):
    return x * (1.0 + mod_ref[0, 1:2, :]) + mod_ref[0, 0:1, :]


def _residual_layer_norm(x, r, alpha, g_ref, b_ref):
    z = alpha * x + r
    mu = jnp.mean(z, axis=-1, keepdims=True)
    zc = z - mu
    var = jnp.mean(zc * zc, axis=-1, keepdims=True)
    return zc * lax.rsqrt(var + NORM_EPS) * g_ref[...] + b_ref[...]


def _ada_kernel(c_ref, w_ref, b_ref, o_ref):
    c = c_ref[...]
    ca = (c * _sigmoid(c)).astype(BF16)
    o_ref[0] = jnp.dot(ca, w_ref[0].astype(BF16), preferred_element_type=F32) + b_ref[0]


def _ada_mod(c, w_ada, b_ada):
    depth, d, n = w_ada.shape
    bsz = c.shape[0]
    tn = _tile(n, 1024)
    return pl.pallas_call(
        _ada_kernel,
        out_shape=jax.ShapeDtypeStruct((depth, bsz, n), F32),
        grid=(depth, n // tn),
        in_specs=[
            pl.BlockSpec((bsz, d), lambda l, j: (0, 0)),
            pl.BlockSpec((1, d, tn), lambda l, j: (l, 0, j)),
            pl.BlockSpec((1, 1, tn), lambda l, j: (l, 0, j)),
        ],
        out_specs=pl.BlockSpec((1, bsz, tn), lambda l, j: (l, 0, j)),
        compiler_params=_cparams(("parallel", "parallel")),
        name="ada_mod",
    )(c, w_ada, b_ada.reshape(depth, 1, n))


def _rope_kernel(pcol_ref, prow_ref, frow_ref, fcol_ref, cos_ref, sin_ref, cost_ref, sint_ref):
    ang = pcol_ref[0].astype(F32) * frow_ref[...]
    cos_ref[0] = jnp.cos(ang)
    sin_ref[0] = jnp.sin(ang)
    ang_t = fcol_ref[...] * prow_ref[0].astype(F32)
    cost_ref[0] = jnp.cos(ang_t)
    sint_ref[0] = jnp.sin(ang_t)


def _rope_tables(positions):
    bsz, seq = positions.shape
    ts = _tile(seq, 1024)
    inv_freq = (ROPE_THETA ** (-np.arange(MLA_HALF, dtype=np.float32) / MLA_HALF)).astype(np.float32)
    nat = jax.ShapeDtypeStruct((bsz, seq, MLA_HALF), F32)
    tra = jax.ShapeDtypeStruct((bsz, MLA_HALF, seq), F32)
    return pl.pallas_call(
        _rope_kernel,
        out_shape=(nat, nat, tra, tra),
        grid=(bsz, seq // ts),
        in_specs=[
            pl.BlockSpec((1, ts, 1), lambda b, i: (b, i, 0)),
            pl.BlockSpec((1, 1, ts), lambda b, i: (b, 0, i)),
            pl.BlockSpec((1, MLA_HALF), lambda b, i: (0, 0)),
            pl.BlockSpec((MLA_HALF, 1), lambda b, i: (0, 0)),
        ],
        out_specs=(pl.BlockSpec((1, ts, MLA_HALF), lambda b, i: (b, i, 0)),
                   pl.BlockSpec((1, ts, MLA_HALF), lambda b, i: (b, i, 0)),
                   pl.BlockSpec((1, MLA_HALF, ts), lambda b, i: (b, 0, i)),
                   pl.BlockSpec((1, MLA_HALF, ts), lambda b, i: (b, 0, i))),
        compiler_params=_cparams(("parallel", "parallel")),
        name="rope_tables",
    )(positions.reshape(bsz, seq, 1), positions.reshape(bsz, 1, seq),
      jnp.asarray(inv_freq[None, :]), jnp.asarray(inv_freq[:, None]))


def _ffn_kernel(x_ref, mod_ref, wig_ref, wiu_ref, wo_ref, g_ref, b_ref, o_ref, *, alpha, nf):
    j = pl.program_id(2)
    half = x_ref.shape[1] // 2
    gate = FFN_RES_WEIGHT * (1.0 + mod_ref[0, 2:3, :])

    def accumulate(r, first):
        rows = pl.ds(r * half, half)
        u = _modulate(x_ref[0, rows, :], mod_ref).astype(BF16)
        g = jnp.dot(u, wig_ref[...], preferred_element_type=F32)
        up = jnp.dot(u, wiu_ref[...], preferred_element_type=F32)
        a = (g * _sigmoid(g) * up).astype(BF16)
        y = jnp.dot(a, wo_ref[...], preferred_element_type=F32)
        if first:
            o_ref[0, rows, :] = y
        else:
            o_ref[0, rows, :] += y

    def finish(r):
        for c in range(2):
            rows = pl.ds(r * half + c * (half // 2), half // 2)
            o_ref[0, rows, :] = _residual_layer_norm(x_ref[0, rows, :], o_ref[0, rows, :] * gate,
                                                     alpha, g_ref, b_ref)

    def step(first, last):
        accumulate(0, first)
        accumulate(1, first)
        if last:
            finish(0)
            finish(1)

    if nf == 1:
        step(True, True)
    else:
        pl.when(j == 0)(lambda: step(True, False))
        if nf > 2:
            pl.when((j > 0) & (j < nf - 1))(lambda: step(False, False))
        pl.when(j == nf - 1)(lambda: step(False, True))


def _ffn(x, mod, wi, wo, ln_g, ln_b, l, k, alpha):
    bsz, seq, d = x.shape
    f = wo.shape[1]
    tm = _tile(seq, FFN_ROWS)
    tf = _tile(f, FFN_COLS)
    nf = f // tf
    return pl.pallas_call(
        functools.partial(_ffn_kernel, alpha=alpha, nf=nf),
        out_shape=jax.ShapeDtypeStruct(x.shape, F32),
        grid=(bsz, seq // tm, nf),
        in_specs=[
            pl.BlockSpec((1, tm, d), lambda bb, i, j: (bb, i, 0)),
            _mod_spec(mod, l, k),
            pl.BlockSpec((None, d, tf), lambda bb, i, j: (l, 0, j)),
            pl.BlockSpec((None, d, tf), lambda bb, i, j: (l, 0, j + nf)),
            pl.BlockSpec((None, tf, d), lambda bb, i, j: (l, j, 0)),
            _ln_spec(ln_g, l, k),
            _ln_spec(ln_b, l, k),
        ],
        out_specs=pl.BlockSpec((1, tm, d), lambda bb, i, j: (bb, i, 0)),
        compiler_params=_cparams(("parallel", "parallel", "arbitrary")),
        name="ffn",
    )(x, mod, wi, wi, wo, ln_g, ln_b)


def _inproj_kernel(x_ref, mod_ref, w_ref, a_ref, b_ref, c_ref, d_ref):
    u = _modulate(x_ref[0], mod_ref).astype(BF16)
    o0, o1, o2, o3 = 0, GRP_A, GRP_A + GRP_B, GRP_A + GRP_B + GRP_C
    a_ref[0] = jnp.dot(u, w_ref[:, o0:o1], preferred_element_type=F32).astype(BF16)
    b_ref[0] = jnp.dot(u, w_ref[:, o1:o2], preferred_element_type=F32).astype(BF16)
    c_ref[0] = jnp.dot(u, w_ref[:, o2:o3], preferred_element_type=F32).astype(BF16)
    d_ref[0] = jnp.dot(u, w_ref[:, o3:o3 + GRP_D], preferred_element_type=F32)


def _inproj(x, mod, w_r, l):
    bsz, seq, d = x.shape
    tm = _tile(seq, 512)
    n = w_r.shape[-1]

    def row(w):
        return pl.BlockSpec((1, tm, w), lambda bb, i: (bb, i, 0))

    return pl.pallas_call(
        _inproj_kernel,
        out_shape=(jax.ShapeDtypeStruct((bsz, seq, GRP_A), BF16),
                   jax.ShapeDtypeStruct((bsz, seq, GRP_B), BF16),
                   jax.ShapeDtypeStruct((bsz, seq, GRP_C), BF16),
                   jax.ShapeDtypeStruct((bsz, seq, GRP_D), F32)),
        grid=(bsz, seq // tm),
        in_specs=[
            row(d),
            _mod_spec(mod, l, 1),
            pl.BlockSpec((None, d, n), lambda bb, i: (l, 0, 0), pipeline_mode=pl.Buffered(1)),
        ],
        out_specs=(row(GRP_A), row(GRP_B), row(GRP_C), row(GRP_D)),
        compiler_params=_cparams(("parallel", "parallel")),
        name="inproj",
    )(x, mod, w_r)


ML_PAD = 8


def _mlstm_step(a_ref, d_ref, conv_ref, bias_ref, o_ref, xbuf, c_scr, n_scr, m_scr):
    blk = a_ref.shape[1]
    pad = ML_PAD

    xbuf[pad:pad + blk, :] = a_ref[0, :, 0:2 * ML_W].astype(F32)
    qk = conv_ref[ML_CONV - 1:ML_CONV, :] * xbuf[pad:pad + blk, :]
    for tap in range(ML_CONV - 1):
        sh = ML_CONV - 1 - tap
        qk = qk + conv_ref[tap:tap + 1, :] * xbuf[pad - sh:pad - sh + blk, :]
    qk = qk * _sigmoid(qk)
    xbuf[0:pad, :] = xbuf[blk:blk + pad, :]

    pre = d_ref[0] + bias_ref[...]
    row = lax.broadcasted_iota(jnp.int32, (blk, blk), 0)
    col = lax.broadcasted_iota(jnp.int32, (blk, blk), 1)
    causal = col <= row
    eye = col == row
    tri = jnp.where(causal, 1.0, 0.0).astype(F32)
    bcum = jnp.dot(tri, _log_sigmoid(pre), preferred_element_type=F32,
                   precision=lax.Precision.HIGHEST)

    outs = []
    for h in range(ML_HEADS):
        b_col = bcum[:, D_MLF + h:D_MLF + h + 1]
        r_col = pre[:, D_MLI + h:D_MLI + h + 1] - b_col
        r_row = jnp.sum(jnp.where(eye, r_col, 0.0), axis=0, keepdims=True)
        b_last = b_col[blk - 1:blk, :]
        m_prev = m_scr[h, 0:1, 0:1]
        n_prev = n_scr[h]
        c_prev = c_scr[h]

        q = qk[:, h * ML_DH:(h + 1) * ML_DH]
        k = qk[:, ML_W + h * ML_DH:ML_W + (h + 1) * ML_DH] * (ML_DH ** -0.5)
        v = a_ref[0, :, 2 * ML_W + h * ML_DH:2 * ML_W + (h + 1) * ML_DH]
        og = a_ref[0, :, 3 * ML_W + h * ML_DH:3 * ML_W + (h + 1) * ML_DH].astype(F32)
        qb = q.astype(BF16)

        dmat = jnp.where(causal, b_col + r_row, -jnp.inf)
        m_inter = b_col + m_prev
        m_t = jnp.maximum(m_inter, jnp.max(dmat, axis=1, keepdims=True))
        s = lax.dot_general(qb, k.astype(BF16), (((1,), (1,)), ((), ())),
                            preferred_element_type=F32) * jnp.exp(dmat - m_t)
        inter = jnp.exp(m_inter - m_t)
        num = (jnp.dot(s.astype(BF16), v, preferred_element_type=F32)
               + inter * jnp.dot(qb, c_prev.astype(BF16), preferred_element_type=F32))
        den = (jnp.sum(s, axis=1, keepdims=True)
               + inter * jnp.sum(q * n_prev, axis=1, keepdims=True))
        hh = num / jnp.maximum(jnp.abs(den), jnp.exp(-m_t))
        mu = jnp.mean(hh, axis=-1, keepdims=True)
        hc = hh - mu
        var = jnp.mean(hc * hc, axis=-1, keepdims=True)
        outs.append(_sigmoid(og) * (hc * lax.rsqrt(var + NORM_EPS)))

        a_col = b_last + r_col
        a_max = jnp.max(a_col, axis=0, keepdims=True)
        kw = k * jnp.exp(a_col - a_max)
        c_loc = lax.dot_general(kw.astype(BF16), v, (((0,), (0,)), ((), ())),
                                preferred_element_type=F32)
        n_loc = jnp.sum(kw, axis=0, keepdims=True)
        m_new = jnp.maximum(b_last + m_prev, a_max)
        dec = jnp.exp(b_last + m_prev - m_new)
        inj = jnp.exp(a_max - m_new)
        c_scr[h] = dec * c_prev + inj * c_loc
        n_scr[h] = dec * n_prev + inj * n_loc
        m_scr[h] = jnp.broadcast_to(m_new, m_scr.shape[1:])

    o_ref[0] = jnp.concatenate(outs, axis=-1).astype(o_ref.dtype)


def _gla_step(b_ref, d_ref, wg_ref, bg_ref, o_ref, st_scr):
    blk = b_ref.shape[1]
    ch = GLA_CHUNK

    gate_pre = jnp.dot(d_ref[0].astype(BF16), wg_ref[...],
                       preferred_element_type=F32) + bg_ref[...]
    log_a = _log_sigmoid(gate_pre) / GLA_TAU
    row = lax.broadcasted_iota(jnp.int32, (blk, blk), 0)
    col = lax.broadcasted_iota(jnp.int32, (blk, blk), 1)
    same_chunk = (row // ch) == (col // ch)
    tri = jnp.where((col <= row) & same_chunk, 1.0, 0.0).astype(F32)
    bc = jnp.dot(tri, log_a, preferred_element_type=F32,
                 precision=lax.Precision.HIGHEST)
    e_pos = jnp.exp(bc)
    e_neg = jnp.exp(-bc)
    crow = lax.broadcasted_iota(jnp.int32, (ch, ch), 0)
    ccol = lax.broadcasted_iota(jnp.int32, (ch, ch), 1)
    causal = ccol <= crow

    for sc in range(blk // ch):
        r0, r1 = sc * ch, (sc + 1) * ch
        outs = []
        for h in range(GLA_HEADS):
            k0, k1 = h * GLA_DK, (h + 1) * GLA_DK
            q = b_ref[0, r0:r1, k0:k1].astype(F32) * (GLA_DK ** -0.5)
            k = b_ref[0, r0:r1, GLA_KW + k0:GLA_KW + k1].astype(F32)
            v = b_ref[0, r0:r1, 2 * GLA_KW + h * GLA_DV:2 * GLA_KW + (h + 1) * GLA_DV]
            rg = b_ref[0, r0:r1, 2 * GLA_KW + GLA_VW + h * GLA_DV:
                       2 * GLA_KW + GLA_VW + (h + 1) * GLA_DV].astype(F32)
            bc_h = bc[r0:r1, k0:k1]
            b_last = bc_h[ch - 1:ch, :]
            q_dec = (q * e_pos[r0:r1, k0:k1]).astype(BF16)
            k_inv = (k * e_neg[r0:r1, k0:k1]).astype(BF16)
            attn = lax.dot_general(q_dec, k_inv, (((1,), (1,)), ((), ())),
                                   preferred_element_type=F32)
            attn = jnp.where(causal, attn, 0.0)
            st = st_scr[h]
            o = (jnp.dot(attn.astype(BF16), v, preferred_element_type=F32)
                 + lax.dot_general(q_dec, st.astype(BF16), (((1,), (1,)), ((), ())),
                                   preferred_element_type=F32))
            k_w = (k * jnp.exp(b_last - bc_h)).astype(BF16)
            s_loc_t = lax.dot_general(v, k_w, (((0,), (0,)), ((), ())),
                                      preferred_element_type=F32)
            st_scr[h] = st * jnp.exp(b_last) + s_loc_t
            on = o * lax.rsqrt(jnp.mean(o * o, axis=-1, keepdims=True) + NORM_EPS)
            outs.append(rg * _sigmoid(rg) * on)
        o_ref[0, r0:r1, :] = jnp.concatenate(outs, axis=-1).astype(o_ref.dtype)


def _recurrent_kernel(a_ref, b_ref, d_ref, conv_ref, bias_ref, wg_ref, bg_ref, oml_ref, ogla_ref,
                      xbuf, c_scr, n_scr, m_scr, st_scr):
    @pl.when(pl.program_id(1) == 0)
    def _():
        xbuf[0:ML_PAD, :] = jnp.zeros((ML_PAD, 2 * ML_W), F32)
        c_scr[...] = jnp.zeros_like(c_scr)
        n_scr[...] = jnp.zeros_like(n_scr)
        m_scr[...] = jnp.zeros_like(m_scr)
        st_scr[...] = jnp.zeros_like(st_scr)

    _mlstm_step(a_ref, d_ref, conv_ref, bias_ref, oml_ref, xbuf, c_scr, n_scr, m_scr)
    _gla_step(b_ref, d_ref, wg_ref, bg_ref, ogla_ref, st_scr)


def _recurrent(ag, bg, dg, conv, gate_bias, wg_pad, gla_bg, l):
    bsz, seq, _ = ag.shape
    blk = _tile(seq, REC_BLOCK)

    def row(w):
        return pl.BlockSpec((1, blk, w), lambda bb, c: (bb, c, 0))

    return pl.pallas_call(
        _recurrent_kernel,
        out_shape=(jax.ShapeDtypeStruct((bsz, seq, ML_W), BF16),
                   jax.ShapeDtypeStruct((bsz, seq, GLA_VW), BF16)),
        grid=(bsz, seq // blk),
        in_specs=[row(GRP_A), row(GRP_B), row(GRP_D),
                  _layer_spec(conv, l, 2), _layer_spec(gate_bias, l, 2),
                  _layer_spec(wg_pad, l, 2), _layer_spec(gla_bg, l, 2)],
        out_specs=(row(ML_W), row(GLA_VW)),
        scratch_shapes=[
            pltpu.VMEM((blk + ML_PAD, 2 * ML_W), F32),
            pltpu.VMEM((ML_HEADS, ML_DH, ML_DH), F32),
            pltpu.VMEM((ML_HEADS, 1, ML_DH), F32),
            pltpu.VMEM((ML_HEADS, 1, ML_DH), F32),
            pltpu.VMEM((GLA_HEADS, GLA_DV, GLA_DK), F32),
        ],
        compiler_params=_cparams(("parallel", "arbitrary")),
        name="recurrent",
    )(ag, bg, dg, conv, gate_bias, wg_pad, gla_bg)


def _rms(x, g_ref):
    return x * lax.rsqrt(jnp.mean(x * x, axis=-1, keepdims=True) + NORM_EPS) * g_ref[...]


def _dot_nt(a, b):
    return lax.dot_general(a, b, (((1,), (1,)), ((), ())), preferred_element_type=F32)


def _mla_proj_kernel(c_ref, d_ref, cos_ref, sin_ref, cost_ref, sint_ref, gq_ref, gkv_ref,
                     wqn_ref, wq1_ref, wq2_ref, wuk_ref, wuv_ref, qt_ref, k_ref, vt_ref):
    cq = c_ref[0, :, 0:MLA_RANK].astype(F32)
    ckv = c_ref[0, :, MLA_RANK:2 * MLA_RANK].astype(F32)
    qn = _rms(cq, gq_ref).astype(BF16)
    kvn = _rms(ckv, gkv_ref).astype(BF16)

    q_nope_t = _dot_nt(wqn_ref[...], qn)
    q1_t = _dot_nt(wq1_ref[...], qn)
    q2_t = _dot_nt(wq2_ref[...], qn)
    cos_t = jnp.concatenate([cost_ref[0]] * MLA_HEADS, axis=0)
    sin_t = jnp.concatenate([sint_ref[0]] * MLA_HEADS, axis=0)
    qr1_t = q1_t * cos_t - q2_t * sin_t
    qr2_t = q1_t * sin_t + q2_t * cos_t

    k_nope = jnp.dot(kvn, wuk_ref[...], preferred_element_type=F32)
    v_t = _dot_nt(wuv_ref[...], kvn)
    kr = d_ref[0]
    k1 = kr[:, D_KR:D_KR + MLA_HALF]
    k2 = kr[:, D_KR + MLA_HALF:D_KR + MLA_ROPE]
    cos = cos_ref[0]
    sin = sin_ref[0]
    kr1 = (k1 * cos - k2 * sin).astype(BF16)
    kr2 = (k1 * sin + k2 * cos).astype(BF16)
    for h in range(MLA_HEADS):
        n0, n1 = h * MLA_NOPE, (h + 1) * MLA_NOPE
        h0, h1 = h * MLA_HALF, (h + 1) * MLA_HALF
        q_t = jnp.concatenate([q_nope_t[n0:n1], qr1_t[h0:h1], qr2_t[h0:h1]], axis=0)
        qt_ref[0, h] = (q_t * QK_SCALE_LOG2).astype(BF16)
        k_ref[0, h] = jnp.concatenate([k_nope[:, n0:n1].astype(BF16), kr1, kr2], axis=-1)
        vt_ref[0, h] = v_t[h * MLA_DV:(h + 1) * MLA_DV].astype(BF16)


def _mla_proj(cg, dg, tables, gq, gkv, wqn_t, wq1_t, wq2_t, wuk, wuv_t, l):
    bsz, seq, _ = cg.shape
    tm = _tile(seq, 512)
    cos, sin, cos_t, sin_t = tables

    def row(w):
        return pl.BlockSpec((1, tm, w), lambda bb, i: (bb, i, 0))

    def col(w):
        return pl.BlockSpec((1, w, tm), lambda bb, i: (bb, 0, i))

    return pl.pallas_call(
        _mla_proj_kernel,
        out_shape=(jax.ShapeDtypeStruct((bsz, MLA_HEADS, MLA_QK, seq), BF16),
                   jax.ShapeDtypeStruct((bsz, MLA_HEADS, seq, MLA_QK), BF16),
                   jax.ShapeDtypeStruct((bsz, MLA_HEADS, MLA_DV, seq), BF16)),
        grid=(bsz, seq // tm),
        in_specs=[row(GRP_C), row(GRP_D), row(MLA_HALF), row(MLA_HALF),
                  col(MLA_HALF), col(MLA_HALF),
                  _layer_spec(gq, l, 2), _layer_spec(gkv, l, 2),
                  _layer_spec(wqn_t, l, 2), _layer_spec(wq1_t, l, 2), _layer_spec(wq2_t, l, 2),
                  _layer_spec(wuk, l, 2), _layer_spec(wuv_t, l, 2)],
        out_specs=(pl.BlockSpec((1, MLA_HEADS, MLA_QK, tm), lambda bb, i: (bb, 0, 0, i)),
                   pl.BlockSpec((1, MLA_HEADS, tm, MLA_QK), lambda bb, i: (bb, 0, i, 0)),
                   pl.BlockSpec((1, MLA_HEADS, MLA_DV, tm), lambda bb, i: (bb, 0, 0, i))),
        compiler_params=_cparams(("parallel", "parallel")),
        name="mla_proj",
    )(cg, dg, cos, sin, cos_t, sin_t, gq, gkv, wqn_t, wq1_t, wq2_t, wuk, wuv_t)


def _attn_kernel(qt_ref, k_ref, vt_ref, o_ref, s_ref, m_ref, l_ref, acc_ref, *, tq):
    seq = k_ref.shape[2]
    nq = seq // tq
    key = lax.broadcasted_iota(jnp.int32, (tq, tq), 0)
    qry = lax.broadcasted_iota(jnp.int32, (tq, tq), 1)
    tri = key <= qry
    tasks = [(i, j) for i in range(nq) for j in range(i + 1)]

    def scores(t):
        i, j = tasks[t]
        qt = qt_ref[0, 0, :, i * tq:(i + 1) * tq]
        k = k_ref[0, 0, j * tq:(j + 1) * tq, :]
        s_ref[t % 2] = jnp.dot(k, qt, preferred_element_type=F32)

    def update(t):
        i, j = tasks[t]
        slot = i % 2
        vt = vt_ref[0, 0, :, j * tq:(j + 1) * tq]
        st = s_ref[t % 2]
        if j == i:
            st = jnp.where(tri, st, -jnp.inf)
        if j == 0:
            m_new = jnp.max(st, axis=0, keepdims=True)
            pt = jnp.exp2(st - m_new)
            l_new = jnp.sum(pt, axis=0, keepdims=True)
            acc_new = jnp.dot(vt, pt.astype(BF16), preferred_element_type=F32)
        else:
            m = m_ref[slot]
            m_new = jnp.maximum(m, jnp.max(st, axis=0, keepdims=True))
            pt = jnp.exp2(st - m_new)
            corr = jnp.exp2(m - m_new)
            l_new = corr * l_ref[slot] + jnp.sum(pt, axis=0, keepdims=True)
            acc_new = corr * acc_ref[slot] + jnp.dot(vt, pt.astype(BF16),
                                                     preferred_element_type=F32)
        if j == i:
            o_ref[0, i * tq:(i + 1) * tq, :] = (acc_new / l_new).T.astype(o_ref.dtype)
        else:
            m_ref[slot] = m_new
            l_ref[slot] = l_new
            acc_ref[slot] = acc_new

    scores(0)
    for t in range(len(tasks)):
        if t + 1 < len(tasks):
            scores(t + 1)
        update(t)


def _attention(qt, k, vt):
    bsz, nh, seq, _ = k.shape
    tq = _tile(seq, ATTN_BLOCK)
    return pl.pallas_call(
        functools.partial(_attn_kernel, tq=tq),
        out_shape=jax.ShapeDtypeStruct((bsz, seq, nh * MLA_DV), BF16),
        grid=(bsz, nh),
        in_specs=[
            pl.BlockSpec((1, 1, MLA_QK, seq), lambda bb, h: (bb, h, 0, 0)),
            pl.BlockSpec((1, 1, seq, MLA_QK), lambda bb, h: (bb, h, 0, 0)),
            pl.BlockSpec((1, 1, MLA_DV, seq), lambda bb, h: (bb, h, 0, 0)),
        ],
        out_specs=pl.BlockSpec((1, seq, MLA_DV), lambda bb, h: (bb, 0, h)),
        scratch_shapes=[pltpu.VMEM((2, tq, tq), F32), pltpu.VMEM((2, 1, tq), F32),
                        pltpu.VMEM((2, 1, tq), F32), pltpu.VMEM((2, MLA_DV, tq), F32)],
        compiler_params=_cparams(("parallel", "parallel")),
        name="mla_attn",
    )(qt, k, vt)


def _outproj_kernel(x_ref, mod_ref, yml_ref, ygla_ref, ymla_ref, w_ref, g_ref, b_ref, o_ref,
                    *, alpha):
    y = jnp.concatenate([yml_ref[0], ygla_ref[0], ymla_ref[0]], axis=-1)
    r = jnp.dot(y, w_ref[...], preferred_element_type=F32) * (1.0 + mod_ref[0, 2:3, :])
    o_ref[0] = _residual_layer_norm(x_ref[0], r, alpha, g_ref, b_ref)


def _outproj(x, mod, y_ml, y_gla, y_mla, w_out, ln_g, ln_b, l, alpha):
    bsz, seq, d = x.shape
    tm = _tile(seq, 512)

    def row(w):
        return pl.BlockSpec((1, tm, w), lambda bb, i: (bb, i, 0))

    return pl.pallas_call(
        functools.partial(_outproj_kernel, alpha=alpha),
        out_shape=jax.ShapeDtypeStruct(x.shape, F32),
        grid=(bsz, seq // tm),
        in_specs=[
            row(d),
            _mod_spec(mod, l, 1),
            row(y_ml.shape[-1]), row(y_gla.shape[-1]), row(y_mla.shape[-1]),
            pl.BlockSpec((None,) + w_out.shape[1:], lambda bb, i: (l, 0, 0),
                         pipeline_mode=pl.Buffered(1)),
            _ln_spec(ln_g, l, 1),
            _ln_spec(ln_b, l, 1),
        ],
        out_specs=row(d),
        compiler_params=_cparams(("parallel", "parallel")),
        name="outproj",
    )(x, mod, y_ml, y_gla, y_mla, w_out, ln_g, ln_b)


def _regroup_w_in(w_in):
    sizes = (ML_W, ML_W, ML_W, ML_HEADS, ML_HEADS, ML_W,
             GLA_KW, GLA_KW, GLA_VW, GLA_RANK, GLA_VW,
             MLA_RANK, MLA_RANK, MLA_ROPE)
    offs = np.concatenate([[0], np.cumsum(sizes)])
    (ml_q, ml_k, ml_v, ml_i, ml_f, ml_o, gl_q, gl_k, gl_v, gl_lr, gl_r, c_q, c_kv, k_r) = (
        w_in[..., int(offs[n]):int(offs[n + 1])] for n in range(len(sizes)))
    used = MLA_ROPE + 2 * ML_HEADS + GLA_RANK
    pad = jnp.zeros(w_in.shape[:-1] + (GRP_D - used,), w_in.dtype)
    return jnp.concatenate([ml_q, ml_k, ml_v, ml_o, gl_q, gl_k, gl_v, gl_r, c_q, c_kv,
                            k_r, ml_i, ml_f, gl_lr, pad], axis=-1).astype(BF16)


def kernel(x, c, positions, w_ada, b_ada, ln_g, ln_b, ffn1_wi, ffn1_wo, ffn2_wi, ffn2_wo,
           w_in, ml_conv, ml_bi, ml_bf, gla_wg, gla_bg, mla_gq, mla_wuq, mla_gkv,
           mla_wuk, mla_wuv, w_out):
    depth = w_ada.shape[0]
    bsz, seq, d = x.shape
    alpha = (2.0 * depth) ** 0.25

    mod = _ada_mod(c, w_ada, b_ada).reshape(depth, bsz, 3, 3, d)
    tables = _rope_tables(positions)

    w_in_r = _regroup_w_in(w_in)
    ffn1_wi_b, ffn1_wo_b = ffn1_wi.astype(BF16), ffn1_wo.astype(BF16)
    ffn2_wi_b, ffn2_wo_b = ffn2_wi.astype(BF16), ffn2_wo.astype(BF16)
    w_out_b = w_out.astype(BF16)
    ln_g4 = ln_g.reshape(depth, 3, 1, d)
    ln_b4 = ln_b.reshape(depth, 3, 1, d)
    wuq = mla_wuq.reshape(depth, MLA_RANK, MLA_HEADS, MLA_QK)

    def feat_major(w):
        return jnp.swapaxes(w.reshape(depth, MLA_RANK, -1), 1, 2).astype(BF16)

    wqn_t = feat_major(wuq[..., :MLA_NOPE])
    wq1_t = feat_major(wuq[..., MLA_NOPE:MLA_NOPE + MLA_HALF])
    wq2_t = feat_major(wuq[..., MLA_NOPE + MLA_HALF:])
    wuk_b = mla_wuk.astype(BF16)
    wuv_t = feat_major(mla_wuv)
    gq3 = mla_gq.reshape(depth, 1, MLA_RANK)
    gkv3 = mla_gkv.reshape(depth, 1, MLA_RANK)
    gate_bias = jnp.zeros((depth, 1, GRP_D), F32)
    gate_bias = gate_bias.at[:, 0, D_MLI:D_MLI + ML_HEADS].set(ml_bi)
    gate_bias = gate_bias.at[:, 0, D_MLF:D_MLF + ML_HEADS].set(ml_bf)
    wg_pad = jnp.zeros((depth, GRP_D, GLA_KW), F32)
    wg_pad = wg_pad.at[:, D_GLR:D_GLR + GLA_RANK, :].set(gla_wg).astype(BF16)
    gla_bg3 = gla_bg.reshape(depth, 1, GLA_KW)

    for l in range(depth):
        x = _ffn(x, mod, ffn1_wi_b, ffn1_wo_b, ln_g4, ln_b4, l, 0, alpha)

        ag, bg, cg, dg = _inproj(x, mod, w_in_r, l)
        y_ml, y_gla = _recurrent(ag, bg, dg, ml_conv, gate_bias, wg_pad, gla_bg3, l)
        qt, k, vt = _mla_proj(cg, dg, tables, gq3, gkv3, wqn_t, wq1_t, wq2_t, wuk_b, wuv_t, l)
        y_mla = _attention(qt, k, vt)
        x = _outproj(x, mod, y_ml, y_gla, y_mla, w_out_b, ln_g4, ln_b4, l, alpha)

        x = _ffn(x, mod, ffn2_wi_b, ffn2_wo_b, ln_g4, ln_b4, l, 2, alpha)
    return x
```

```python
import functools
import math

import jax
import jax.numpy as jnp
import numpy as np
from jax import lax
from jax.experimental import pallas as pl
from jax.experimental.pallas import tpu as pltpu

F32 = jnp.float32
BF16 = jnp.bfloat16

ML_HEADS = 4
ML_DH = 128
ML_W = ML_HEADS * ML_DH
ML_CONV = 4
GLA_HEADS = 4
GLA_DK = 64
GLA_DV = 128
GLA_KW = GLA_HEADS * GLA_DK
GLA_VW = GLA_HEADS * GLA_DV
GLA_RANK = 16
GLA_TAU = 16.0
GLA_CHUNK = 64
MLA_HEADS = 8
MLA_NOPE = 128
MLA_ROPE = 64
MLA_HALF = MLA_ROPE // 2
MLA_DV = 128
MLA_QK = MLA_NOPE + MLA_ROPE
MLA_RANK = 512
ROPE_THETA = 10000.0
NORM_EPS = 1e-5
FFN_RES_WEIGHT = 0.5
QK_SCALE_LOG2 = math.log2(math.e) * MLA_QK ** -0.5

GRP_A = 4 * ML_W
GRP_B = 2 * GLA_KW + 2 * GLA_VW
GRP_C = 2 * MLA_RANK
GRP_D = 128
D_KR = 0
D_MLI = MLA_ROPE
D_MLF = D_MLI + ML_HEADS
D_GLR = D_MLF + ML_HEADS

VMEM_LIMIT_V7X = 58 * 1024 * 1024

FFN_ROWS = 1024
FFN_COLS = 512
REC_BLOCK = 256
ATTN_BLOCK = 1024


def _cparams(sem):
    return pltpu.CompilerParams(dimension_semantics=sem, vmem_limit_bytes=VMEM_LIMIT_V7X)


def _tile(n, pref):
    t = min(n, pref)
    while n % t:
        t //= 2
    return t


def _layer_spec(arr, l):
    shape = (None,) + arr.shape[1:]
    zeros = (0,) * (arr.ndim - 1)
    return pl.BlockSpec(shape, lambda *_: (l,) + zeros)


def _mod_spec(mod, l, k):
    d = mod.shape[-1]
    return pl.BlockSpec((None, 1, None, 3, d), lambda bb, *_: (l, bb, k, 0, 0))


def _ln_spec(ln, l, k):
    return pl.BlockSpec((None, None, 1, ln.shape[-1]), lambda *_: (l, k, 0, 0))


def _sigmoid(x):
    return jax.nn.sigmoid(x)


def _log_sigmoid(x):
    return jnp.minimum(x, 0.0) - jnp.log1p(jnp.exp(-jnp.abs(x)))


def _modulate(x, mod_ref):
    return x * (1.0 + mod_ref[0, 1:2, :]) + mod_ref[0, 0:1, :]


def _residual_layer_norm(x, r, alpha, g_ref, b_ref):
    z = alpha * x + r
    mu = jnp.mean(z, axis=-1, keepdims=True)
    zc = z - mu
    var = jnp.mean(zc * zc, axis=-1, keepdims=True)
    return zc * lax.rsqrt(var + NORM_EPS) * g_ref[...] + b_ref[...]


def _ada_kernel(c_ref, w_ref, b_ref, o_ref):
    c = c_ref[...]
    ca = (c * _sigmoid(c)).astype(BF16)
    o_ref[0] = jnp.dot(ca, w_ref[0].astype(BF16), preferred_element_type=F32) + b_ref[0]


def _ada_mod(c, w_ada, b_ada):
    depth, d, n = w_ada.shape
    bsz = c.shape[0]
    tn = _tile(n, 1024)
    return pl.pallas_call(
        _ada_kernel,
        out_shape=jax.ShapeDtypeStruct((depth, bsz, n), F32),
        grid=(depth, n // tn),
        in_specs=[
            pl.BlockSpec((bsz, d), lambda l, j: (0, 0)),
            pl.BlockSpec((1, d, tn), lambda l, j: (l, 0, j)),
            pl.BlockSpec((1, 1, tn), lambda l, j: (l, 0, j)),
        ],
        out_specs=pl.BlockSpec((1, bsz, tn), lambda l, j: (l, 0, j)),
        compiler_params=_cparams(("parallel", "parallel")),
        name="ada_mod",
    )(c, w_ada, b_ada.reshape(depth, 1, n))


def _rope_kernel(pcol_ref, prow_ref, frow_ref, fcol_ref, cos_ref, sin_ref, cost_ref, sint_ref):
    ang = pcol_ref[0].astype(F32) * frow_ref[...]
    cos_ref[0] = jnp.cos(ang)
    sin_ref[0] = jnp.sin(ang)
    ang_t = fcol_ref[...] * prow_ref[0].astype(F32)
    cost_ref[0] = jnp.cos(ang_t)
    sint_ref[0] = jnp.sin(ang_t)


def _rope_tables(positions):
    bsz, seq = positions.shape
    ts = _tile(seq, 1024)
    inv_freq = (ROPE_THETA ** (-np.arange(MLA_HALF, dtype=np.float32) / MLA_HALF)).astype(np.float32)
    nat = jax.ShapeDtypeStruct((bsz, seq, MLA_HALF), F32)
    tra = jax.ShapeDtypeStruct((bsz, MLA_HALF, seq), F32)
    return pl.pallas_call(
        _rope_kernel,
        out_shape=(nat, nat, tra, tra),
        grid=(bsz, seq // ts),
        in_specs=[
            pl.BlockSpec((1, ts, 1), lambda b, i: (b, i, 0)),
            pl.BlockSpec((1, 1, ts), lambda b, i: (b, 0, i)),
            pl.BlockSpec((1, MLA_HALF), lambda b, i: (0, 0)),
            pl.BlockSpec((MLA_HALF, 1), lambda b, i: (0, 0)),
        ],
        out_specs=(pl.BlockSpec((1, ts, MLA_HALF), lambda b, i: (b, i, 0)),
                   pl.BlockSpec((1, ts, MLA_HALF), lambda b, i: (b, i, 0)),
                   pl.BlockSpec((1, MLA_HALF, ts), lambda b, i: (b, 0, i)),
                   pl.BlockSpec((1, MLA_HALF, ts), lambda b, i: (b, 0, i))),
        compiler_params=_cparams(("parallel", "parallel")),
        name="rope_tables",
    )(positions.reshape(bsz, seq, 1), positions.reshape(bsz, 1, seq),
      jnp.asarray(inv_freq[None, :]), jnp.asarray(inv_freq[:, None]))


def _ffn_kernel(x_ref, mod_ref, wig_ref, wiu_ref, wo_ref, g_ref, b_ref, o_ref, *, alpha, nf):
    j = pl.program_id(2)
    half = x_ref.shape[1] // 2
    gate = FFN_RES_WEIGHT * (1.0 + mod_ref[0, 2:3, :])

    def accumulate(r, first):
        rows = pl.ds(r * half, half)
        u = _modulate(x_ref[0, rows, :], mod_ref).astype(BF16)
        g = jnp.dot(u, wig_ref[...], preferred_element_type=F32)
        up = jnp.dot(u, wiu_ref[...], preferred_element_type=F32)
        a = (g * _sigmoid(g) * up).astype(BF16)
        y = jnp.dot(a, wo_ref[...], preferred_element_type=F32)
        if first:
            o_ref[0, rows, :] = y
        else:
            o_ref[0, rows, :] += y

    def finish(r):
        for c in range(2):
            rows = pl.ds(r * half + c * (half // 2), half // 2)
            o_ref[0, rows, :] = _residual_layer_norm(x_ref[0, rows, :], o_ref[0, rows, :] * gate,
                                                     alpha, g_ref, b_ref)

    def step(first, last):
        accumulate(0, first)
        accumulate(1, first)
        if last:
            finish(0)
            finish(1)

    if nf == 1:
        step(True, True)
    else:
        pl.when(j == 0)(lambda: step(True, False))
        if nf > 2:
            pl.when((j > 0) & (j < nf - 1))(lambda: step(False, False))
        pl.when(j == nf - 1)(lambda: step(False, True))


def _ffn(x, mod, wi, wo, ln_g, ln_b, l, k, alpha):
    bsz, seq, d = x.shape
    f = wo.shape[1]
    tm = _tile(seq, FFN_ROWS)
    tf = _tile(f, FFN_COLS)
    nf = f // tf
    return pl.pallas_call(
        functools.partial(_ffn_kernel, alpha=alpha, nf=nf),
        out_shape=jax.ShapeDtypeStruct(x.shape, F32),
        grid=(bsz, seq // tm, nf),
        in_specs=[
            pl.BlockSpec((1, tm, d), lambda bb, i, j: (bb, i, 0)),
            _mod_spec(mod, l, k),
            pl.BlockSpec((None, d, tf), lambda bb, i, j: (l, 0, j)),
            pl.BlockSpec((None, d, tf), lambda bb, i, j: (l, 0, j + nf)),
            pl.BlockSpec((None, tf, d), lambda bb, i, j: (l, j, 0)),
            _ln_spec(ln_g, l, k),
            _ln_spec(ln_b, l, k),
        ],
        out_specs=pl.BlockSpec((1, tm, d), lambda bb, i, j: (bb, i, 0)),
        compiler_params=_cparams(("parallel", "parallel", "arbitrary")),
        name="ffn",
    )(x, mod, wi, wi, wo, ln_g, ln_b)


def _inproj_kernel(x_ref, mod_ref, w_ref, a_ref, b_ref, c_ref, d_ref):
    u = _modulate(x_ref[0], mod_ref).astype(BF16)
    o0, o1, o2, o3 = 0, GRP_A, GRP_A + GRP_B, GRP_A + GRP_B + GRP_C
    a_ref[0] = jnp.dot(u, w_ref[:, o0:o1], preferred_element_type=F32).astype(BF16)
    b_ref[0] = jnp.dot(u, w_ref[:, o1:o2], preferred_element_type=F32).astype(BF16)
    c_ref[0] = jnp.dot(u, w_ref[:, o2:o3], preferred_element_type=F32).astype(BF16)
    d_ref[0] = jnp.dot(u, w_ref[:, o3:o3 + GRP_D], preferred_element_type=F32)


def _inproj(x, mod, w_r, l):
    bsz, seq, d = x.shape
    tm = _tile(seq, 512)
    n = w_r.shape[-1]

    def row(w):
        return pl.BlockSpec((1, tm, w), lambda bb, i: (bb, i, 0))

    return pl.pallas_call(
        _inproj_kernel,
        out_shape=(jax.ShapeDtypeStruct((bsz, seq, GRP_A), BF16),
                   jax.ShapeDtypeStruct((bsz, seq, GRP_B), BF16),
                   jax.ShapeDtypeStruct((bsz, seq, GRP_C), BF16),
                   jax.ShapeDtypeStruct((bsz, seq, GRP_D), F32)),
        grid=(bsz, seq // tm),
        in_specs=[
            row(d),
            _mod_spec(mod, l, 1),
            pl.BlockSpec((None, d, n), lambda bb, i: (l, 0, 0), pipeline_mode=pl.Buffered(1)),
        ],
        out_specs=(row(GRP_A), row(GRP_B), row(GRP_C), row(GRP_D)),
        compiler_params=_cparams(("parallel", "parallel")),
        name="inproj",
    )(x, mod, w_r)


ML_PAD = 8


def _mlstm_step(a_ref, d_ref, conv_ref, bias_ref, o_ref, xbuf, c_scr, n_scr, m_scr):
    blk = a_ref.shape[1]
    pad = ML_PAD

    xbuf[pad:pad + blk, :] = a_ref[0, :, 0:2 * ML_W].astype(F32)
    qk = conv_ref[ML_CONV - 1:ML_CONV, :] * xbuf[pad:pad + blk, :]
    for tap in range(ML_CONV - 1):
        sh = ML_CONV - 1 - tap
        qk = qk + conv_ref[tap:tap + 1, :] * xbuf[pad - sh:pad - sh + blk, :]
    qk = qk * _sigmoid(qk)
    xbuf[0:pad, :] = xbuf[blk:blk + pad, :]

    pre = d_ref[0] + bias_ref[...]
    row = lax.broadcasted_iota(jnp.int32, (blk, blk), 0)
    col = lax.broadcasted_iota(jnp.int32, (blk, blk), 1)
    causal = col <= row
    eye = col == row
    tri = jnp.where(causal, 1.0, 0.0).astype(F32)
    bcum = jnp.dot(tri, _log_sigmoid(pre), preferred_element_type=F32,
                   precision=lax.Precision.HIGHEST)

    outs = []
    for h in range(ML_HEADS):
        b_col = bcum[:, D_MLF + h:D_MLF + h + 1]
        r_col = pre[:, D_MLI + h:D_MLI + h + 1] - b_col
        r_row = jnp.sum(jnp.where(eye, r_col, 0.0), axis=0, keepdims=True)
        b_last = b_col[blk - 1:blk, :]
        m_prev = m_scr[h, 0:1, 0:1]
        n_prev = n_scr[h]
        c_prev = c_scr[h]

        q = qk[:, h * ML_DH:(h + 1) * ML_DH]
        k = qk[:, ML_W + h * ML_DH:ML_W + (h + 1) * ML_DH] * (ML_DH ** -0.5)
        v = a_ref[0, :, 2 * ML_W + h * ML_DH:2 * ML_W + (h + 1) * ML_DH]
        og = a_ref[0, :, 3 * ML_W + h * ML_DH:3 * ML_W + (h + 1) * ML_DH].astype(F32)
        qb = q.astype(BF16)

        dmat = jnp.where(causal, b_col + r_row, -jnp.inf)
        m_inter = b_col + m_prev
        m_t = jnp.maximum(m_inter, jnp.max(dmat, axis=1, keepdims=True))
        s = lax.dot_general(qb, k.astype(BF16), (((1,), (1,)), ((), ())),
                            preferred_element_type=F32) * jnp.exp(dmat - m_t)
        inter = jnp.exp(m_inter - m_t)
        num = (jnp.dot(s.astype(BF16), v, preferred_element_type=F32)
               + inter * jnp.dot(qb, c_prev.astype(BF16), preferred_element_type=F32))
        den = (jnp.sum(s, axis=1, keepdims=True)
               + inter * jnp.sum(q * n_prev, axis=1, keepdims=True))
        hh = num / jnp.maximum(jnp.abs(den), jnp.exp(-m_t))
        mu = jnp.mean(hh, axis=-1, keepdims=True)
        hc = hh - mu
        var = jnp.mean(hc * hc, axis=-1, keepdims=True)
        outs.append(_sigmoid(og) * (hc * lax.rsqrt(var + NORM_EPS)))

        a_col = b_last + r_col
        a_max = jnp.max(a_col, axis=0, keepdims=True)
        kw = k * jnp.exp(a_col - a_max)
        c_loc = lax.dot_general(kw.astype(BF16), v, (((0,), (0,)), ((), ())),
                                preferred_element_type=F32)
        n_loc = jnp.sum(kw, axis=0, keepdims=True)
        m_new = jnp.maximum(b_last + m_prev, a_max)
        dec = jnp.exp(b_last + m_prev - m_new)
        inj = jnp.exp(a_max - m_new)
        c_scr[h] = dec * c_prev + inj * c_loc
        n_scr[h] = dec * n_prev + inj * n_loc
        m_scr[h] = jnp.broadcast_to(m_new, m_scr.shape[1:])

    o_ref[0] = jnp.concatenate(outs, axis=-1).astype(o_ref.dtype)


def _gla_step(b_ref, d_ref, wg_ref, bg_ref, o_ref, st_scr):
    blk = b_ref.shape[1]
    ch = GLA_CHUNK

    gate_pre = jnp.dot(d_ref[0].astype(BF16), wg_ref[...],
                       preferred_element_type=F32) + bg_ref[...]
    log_a = _log_sigmoid(gate_pre) / GLA_TAU
    row = lax.broadcasted_iota(jnp.int32, (blk, blk), 0)
    col = lax.broadcasted_iota(jnp.int32, (blk, blk), 1)
    same_chunk = (row // ch) == (col // ch)
    tri = jnp.where((col <= row) & same_chunk, 1.0, 0.0).astype(F32)
    bc = jnp.dot(tri, log_a, preferred_element_type=F32,
                 precision=lax.Precision.HIGHEST)
    e_pos = jnp.exp(bc)
    e_neg = jnp.exp(-bc)
    crow = lax.broadcasted_iota(jnp.int32, (ch, ch), 0)
    ccol = lax.broadcasted_iota(jnp.int32, (ch, ch), 1)
    causal = ccol <= crow

    for sc in range(blk // ch):
        r0, r1 = sc * ch, (sc + 1) * ch
        outs = []
        for h in range(GLA_HEADS):
            k0, k1 = h * GLA_DK, (h + 1) * GLA_DK
            q = b_ref[0, r0:r1, k0:k1].astype(F32) * (GLA_DK ** -0.5)
            k = b_ref[0, r0:r1, GLA_KW + k0:GLA_KW + k1].astype(F32)
            v = b_ref[0, r0:r1, 2 * GLA_KW + h * GLA_DV:2 * GLA_KW + (h + 1) * GLA_DV]
            rg = b_ref[0, r0:r1, 2 * GLA_KW + GLA_VW + h * GLA_DV:
                       2 * GLA_KW + GLA_VW + (h + 1) * GLA_DV].astype(F32)
            bc_h = bc[r0:r1, k0:k1]
            b_last = bc_h[ch - 1:ch, :]
            q_dec = (q * e_pos[r0:r1, k0:k1]).astype(BF16)
            k_inv = (k * e_neg[r0:r1, k0:k1]).astype(BF16)
            attn = lax.dot_general(q_dec, k_inv, (((1,), (1,)), ((), ())),
                                   preferred_element_type=F32)
            attn = jnp.where(causal, attn, 0.0)
            st = st_scr[h]
            o = (jnp.dot(attn.astype(BF16), v, preferred_element_type=F32)
                 + lax.dot_general(q_dec, st.astype(BF16), (((1,), (1,)), ((), ())),
                                   preferred_element_type=F32))
            k_w = (k * jnp.exp(b_last - bc_h)).astype(BF16)
            s_loc_t = lax.dot_general(v, k_w, (((0,), (0,)), ((), ())),
                                      preferred_element_type=F32)
            st_scr[h] = st * jnp.exp(b_last) + s_loc_t
            on = o * lax.rsqrt(jnp.mean(o * o, axis=-1, keepdims=True) + NORM_EPS)
            outs.append(rg * _sigmoid(rg) * on)
        o_ref[0, r0:r1, :] = jnp.concatenate(outs, axis=-1).astype(o_ref.dtype)


def _recurrent_kernel(a_ref, b_ref, d_ref, conv_ref, bias_ref, wg_ref, bg_ref, oml_ref, ogla_ref,
                      xbuf, c_scr, n_scr, m_scr, st_scr):
    @pl.when(pl.program_id(1) == 0)
    def _():
        xbuf[0:ML_PAD, :] = jnp.zeros((ML_PAD, 2 * ML_W), F32)
        c_scr[...] = jnp.zeros_like(c_scr)
        n_scr[...] = jnp.zeros_like(n_scr)
        m_scr[...] = jnp.zeros_like(m_scr)
        st_scr[...] = jnp.zeros_like(st_scr)

    _mlstm_step(a_ref, d_ref, conv_ref, bias_ref, oml_ref, xbuf, c_scr, n_scr, m_scr)
    _gla_step(b_ref, d_ref, wg_ref, bg_ref, ogla_ref, st_scr)


def _recurrent(ag, bg, dg, conv, gate_bias, wg_pad, gla_bg, l):
    bsz, seq, _ = ag.shape
    blk = _tile(seq, REC_BLOCK)

    def row(w):
        return pl.BlockSpec((1, blk, w), lambda bb, c: (bb, c, 0))

    return pl.pallas_call(
        _recurrent_kernel,
        out_shape=(jax.ShapeDtypeStruct((bsz, seq, ML_W), BF16),
                   jax.ShapeDtypeStruct((bsz, seq, GLA_VW), BF16)),
        grid=(bsz, seq // blk),
        in_specs=[row(GRP_A), row(GRP_B), row(GRP_D),
                  _layer_spec(conv, l), _layer_spec(gate_bias, l),
                  _layer_spec(wg_pad, l), _layer_spec(gla_bg, l)],
        out_specs=(row(ML_W), row(GLA_VW)),
        scratch_shapes=[
            pltpu.VMEM((blk + ML_PAD, 2 * ML_W), F32),
            pltpu.VMEM((ML_HEADS, ML_DH, ML_DH), F32),
            pltpu.VMEM((ML_HEADS, 1, ML_DH), F32),
            pltpu.VMEM((ML_HEADS, 1, ML_DH), F32),
            pltpu.VMEM((GLA_HEADS, GLA_DV, GLA_DK), F32),
        ],
        compiler_params=_cparams(("parallel", "arbitrary")),
        name="recurrent",
    )(ag, bg, dg, conv, gate_bias, wg_pad, gla_bg)


def _rms(x, g_ref):
    return x * lax.rsqrt(jnp.mean(x * x, axis=-1, keepdims=True) + NORM_EPS) * g_ref[...]


def _dot_nt(a, b):
    return lax.dot_general(a, b, (((1,), (1,)), ((), ())), preferred_element_type=F32)


def _mla_proj_kernel(c_ref, d_ref, cos_ref, sin_ref, cost_ref, sint_ref, gq_ref, gkv_ref,
                     wqn_ref, wq1_ref, wq2_ref, wuk_ref, wuv_ref, qt_ref, k_ref, vt_ref):
    cq = c_ref[0, :, 0:MLA_RANK].astype(F32)
    ckv = c_ref[0, :, MLA_RANK:2 * MLA_RANK].astype(F32)
    qn = _rms(cq, gq_ref).astype(BF16)
    kvn = _rms(ckv, gkv_ref).astype(BF16)

    q_nope_t = _dot_nt(wqn_ref[...], qn)
    q1_t = _dot_nt(wq1_ref[...], qn)
    q2_t = _dot_nt(wq2_ref[...], qn)
    cos_t = jnp.concatenate([cost_ref[0]] * MLA_HEADS, axis=0)
    sin_t = jnp.concatenate([sint_ref[0]] * MLA_HEADS, axis=0)
    qr1_t = q1_t * cos_t - q2_t * sin_t
    qr2_t = q1_t * sin_t + q2_t * cos_t

    k_nope = jnp.dot(kvn, wuk_ref[...], preferred_element_type=F32)
    v_t = _dot_nt(wuv_ref[...], kvn)
    kr = d_ref[0]
    k1 = kr[:, D_KR:D_KR + MLA_HALF]
    k2 = kr[:, D_KR + MLA_HALF:D_KR + MLA_ROPE]
    cos = cos_ref[0]
    sin = sin_ref[0]
    kr1 = (k1 * cos - k2 * sin).astype(BF16)
    kr2 = (k1 * sin + k2 * cos).astype(BF16)
    for h in range(MLA_HEADS):
        n0, n1 = h * MLA_NOPE, (h + 1) * MLA_NOPE
        h0, h1 = h * MLA_HALF, (h + 1) * MLA_HALF
        q_t = jnp.concatenate([q_nope_t[n0:n1], qr1_t[h0:h1], qr2_t[h0:h1]], axis=0)
        qt_ref[0, h] = (q_t * QK_SCALE_LOG2).astype(BF16)
        k_ref[0, h] = jnp.concatenate([k_nope[:, n0:n1].astype(BF16), kr1, kr2], axis=-1)
        vt_ref[0, h] = v_t[h * MLA_DV:(h + 1) * MLA_DV].astype(BF16)


def _mla_proj(cg, dg, tables, gq, gkv, wqn_t, wq1_t, wq2_t, wuk, wuv_t, l):
    bsz, seq, _ = cg.shape
    tm = _tile(seq, 512)
    cos, sin, cos_t, sin_t = tables

    def row(w):
        return pl.BlockSpec((1, tm, w), lambda bb, i: (bb, i, 0))

    def col(w):
        return pl.BlockSpec((1, w, tm), lambda bb, i: (bb, 0, i))

    return pl.pallas_call(
        _mla_proj_kernel,
        out_shape=(jax.ShapeDtypeStruct((bsz, MLA_HEADS, MLA_QK, seq), BF16),
                   jax.ShapeDtypeStruct((bsz, MLA_HEADS, seq, MLA_QK), BF16),
                   jax.ShapeDtypeStruct((bsz, MLA_HEADS, MLA_DV, seq), BF16)),
        grid=(bsz, seq // tm),
        in_specs=[row(GRP_C), row(GRP_D), row(MLA_HALF), row(MLA_HALF),
                  col(MLA_HALF), col(MLA_HALF),
                  _layer_spec(gq, l), _layer_spec(gkv, l),
                  _layer_spec(wqn_t, l), _layer_spec(wq1_t, l), _layer_spec(wq2_t, l),
                  _layer_spec(wuk, l), _layer_spec(wuv_t, l)],
        out_specs=(pl.BlockSpec((1, MLA_HEADS, MLA_QK, tm), lambda bb, i: (bb, 0, 0, i)),
                   pl.BlockSpec((1, MLA_HEADS, tm, MLA_QK), lambda bb, i: (bb, 0, i, 0)),
                   pl.BlockSpec((1, MLA_HEADS, MLA_DV, tm), lambda bb, i: (bb, 0, 0, i))),
        compiler_params=_cparams(("parallel", "parallel")),
        name="mla_proj",
    )(cg, dg, cos, sin, cos_t, sin_t, gq, gkv, wqn_t, wq1_t, wq2_t, wuk, wuv_t)


def _attn_kernel(qt_ref, k_ref, vt_ref, o_ref, s_ref, m_ref, l_ref, acc_ref, *, tq):
    seq = k_ref.shape[2]
    nq = seq // tq
    key = lax.broadcasted_iota(jnp.int32, (tq, tq), 0)
    qry = lax.broadcasted_iota(jnp.int32, (tq, tq), 1)
    tri = key <= qry
    tasks = [(i, j) for i in range(nq) for j in range(i + 1)]

    def scores(t):
        i, j = tasks[t]
        qt = qt_ref[0, 0, :, i * tq:(i + 1) * tq]
        k = k_ref[0, 0, j * tq:(j + 1) * tq, :]
        s_ref[t % 2] = jnp.dot(k, qt, preferred_element_type=F32)

    def update(t):
        i, j = tasks[t]
        slot = i % 2
        vt = vt_ref[0, 0, :, j * tq:(j + 1) * tq]
        st = s_ref[t % 2]
        if j == i:
            st = jnp.where(tri, st, -jnp.inf)
        if j == 0:
            m_new = jnp.max(st, axis=0, keepdims=True)
            pt = jnp.exp2(st - m_new)
            l_new = jnp.sum(pt, axis=0, keepdims=True)
            acc_new = jnp.dot(vt, pt.astype(BF16), preferred_element_type=F32)
        else:
            m = m_ref[slot]
            m_new = jnp.maximum(m, jnp.max(st, axis=0, keepdims=True))
            pt = jnp.exp2(st - m_new)
            corr = jnp.exp2(m - m_new)
            l_new = corr * l_ref[slot] + jnp.sum(pt, axis=0, keepdims=True)
            acc_new = corr * acc_ref[slot] + jnp.dot(vt, pt.astype(BF16),
                                                     preferred_element_type=F32)
        if j == i:
            o_ref[0, i * tq:(i + 1) * tq, :] = (acc_new / l_new).T.astype(o_ref.dtype)
        else:
            m_ref[slot] = m_new
            l_ref[slot] = l_new
            acc_ref[slot] = acc_new

    scores(0)
    for t in range(len(tasks)):
        if t + 1 < len(tasks):
            scores(t + 1)
        update(t)


def _attention(qt, k, vt):
    bsz, nh, seq, _ = k.shape
    tq = _tile(seq, ATTN_BLOCK)
    return pl.pallas_call(
        functools.partial(_attn_kernel, tq=tq),
        out_shape=jax.ShapeDtypeStruct((bsz, seq, nh * MLA_DV), BF16),
        grid=(bsz, nh),
        in_specs=[
            pl.BlockSpec((1, 1, MLA_QK, seq), lambda bb, h: (bb, h, 0, 0)),
            pl.BlockSpec((1, 1, seq, MLA_QK), lambda bb, h: (bb, h, 0, 0)),
            pl.BlockSpec((1, 1, MLA_DV, seq), lambda bb, h: (bb, h, 0, 0)),
        ],
        out_specs=pl.BlockSpec((1, seq, MLA_DV), lambda bb, h: (bb, 0, h)),
        scratch_shapes=[pltpu.VMEM((2, tq, tq), F32), pltpu.VMEM((2, 1, tq), F32),
                        pltpu.VMEM((2, 1, tq), F32), pltpu.VMEM((2, MLA_DV, tq), F32)],
        compiler_params=_cparams(("parallel", "parallel")),
        name="mla_attn",
    )(qt, k, vt)


def _outproj_kernel(x_ref, mod_ref, yml_ref, ygla_ref, ymla_ref, w_ref, g_ref, b_ref, o_ref,
                    *, alpha):
    gate = 1.0 + mod_ref[0, 2:3, :]
    half = x_ref.shape[1] // 2
    for r in range(2):
        rows = pl.ds(r * half, half)
        y = jnp.concatenate([yml_ref[0, rows, :], ygla_ref[0, rows, :], ymla_ref[0, rows, :]],
                            axis=-1)
        res = jnp.dot(y, w_ref[...], preferred_element_type=F32) * gate
        o_ref[0, rows, :] = _residual_layer_norm(x_ref[0, rows, :], res, alpha, g_ref, b_ref)


def _outproj(x, mod, y_ml, y_gla, y_mla, w_out, ln_g, ln_b, l, alpha):
    bsz, seq, d = x.shape
    tm = _tile(seq, 512)

    def row(w):
        return pl.BlockSpec((1, tm, w), lambda bb, i: (bb, i, 0))

    return pl.pallas_call(
        functools.partial(_outproj_kernel, alpha=alpha),
        out_shape=jax.ShapeDtypeStruct(x.shape, F32),
        grid=(bsz, seq // tm),
        in_specs=[
            row(d),
            _mod_spec(mod, l, 1),
            row(y_ml.shape[-1]), row(y_gla.shape[-1]), row(y_mla.shape[-1]),
            pl.BlockSpec((None,) + w_out.shape[1:], lambda bb, i: (l, 0, 0),
                         pipeline_mode=pl.Buffered(1)),
            _ln_spec(ln_g, l, 1),
            _ln_spec(ln_b, l, 1),
        ],
        out_specs=row(d),
        compiler_params=_cparams(("parallel", "parallel")),
        name="outproj",
    )(x, mod, y_ml, y_gla, y_mla, w_out, ln_g, ln_b)


def _regroup_w_in(w_in):
    sizes = (ML_W, ML_W, ML_W, ML_HEADS, ML_HEADS, ML_W,
             GLA_KW, GLA_KW, GLA_VW, GLA_RANK, GLA_VW,
             MLA_RANK, MLA_RANK, MLA_ROPE)
    offs = np.concatenate([[0], np.cumsum(sizes)])
    (ml_q, ml_k, ml_v, ml_i, ml_f, ml_o, gl_q, gl_k, gl_v, gl_lr, gl_r, c_q, c_kv, k_r) = (
        w_in[..., int(offs[n]):int(offs[n + 1])] for n in range(len(sizes)))
    used = MLA_ROPE + 2 * ML_HEADS + GLA_RANK
    pad = jnp.zeros(w_in.shape[:-1] + (GRP_D - used,), w_in.dtype)
    return jnp.concatenate([ml_q, ml_k, ml_v, ml_o, gl_q, gl_k, gl_v, gl_r, c_q, c_kv,
                            k_r, ml_i, ml_f, gl_lr, pad], axis=-1).astype(BF16)


def kernel(x, c, positions, w_ada, b_ada, ln_g, ln_b, ffn1_wi, ffn1_wo, ffn2_wi, ffn2_wo,
           w_in, ml_conv, ml_bi, ml_bf, gla_wg, gla_bg, mla_gq, mla_wuq, mla_gkv,
           mla_wuk, mla_wuv, w_out):
    depth = w_ada.shape[0]
    bsz, seq, d = x.shape
    alpha = (2.0 * depth) ** 0.25

    mod = _ada_mod(c, w_ada, b_ada).reshape(depth, bsz, 3, 3, d)
    tables = _rope_tables(positions)

    w_in_r = _regroup_w_in(w_in)
    ffn1_wi_b, ffn1_wo_b = ffn1_wi.astype(BF16), ffn1_wo.astype(BF16)
    ffn2_wi_b, ffn2_wo_b = ffn2_wi.astype(BF16), ffn2_wo.astype(BF16)
    w_out_b = w_out.astype(BF16)
    ln_g4 = ln_g.reshape(depth, 3, 1, d)
    ln_b4 = ln_b.reshape(depth, 3, 1, d)
    wuq = mla_wuq.reshape(depth, MLA_RANK, MLA_HEADS, MLA_QK)

    def feat_major(w):
        return jnp.swapaxes(w.reshape(depth, MLA_RANK, -1), 1, 2).astype(BF16)

    wqn_t = feat_major(wuq[..., :MLA_NOPE])
    wq1_t = feat_major(wuq[..., MLA_NOPE:MLA_NOPE + MLA_HALF])
    wq2_t = feat_major(wuq[..., MLA_NOPE + MLA_HALF:])
    wuk_b = mla_wuk.astype(BF16)
    wuv_t = feat_major(mla_wuv)
    gq3 = mla_gq.reshape(depth, 1, MLA_RANK)
    gkv3 = mla_gkv.reshape(depth, 1, MLA_RANK)
    gate_bias = jnp.zeros((depth, 1, GRP_D), F32)
    gate_bias = gate_bias.at[:, 0, D_MLI:D_MLI + ML_HEADS].set(ml_bi)
    gate_bias = gate_bias.at[:, 0, D_MLF:D_MLF + ML_HEADS].set(ml_bf)
    wg_pad = jnp.zeros((depth, GRP_D, GLA_KW), F32)
    wg_pad = wg_pad.at[:, D_GLR:D_GLR + GLA_RANK, :].set(gla_wg).astype(BF16)
    gla_bg3 = gla_bg.reshape(depth, 1, GLA_KW)

    for l in range(depth):
        x = _ffn(x, mod, ffn1_wi_b, ffn1_wo_b, ln_g4, ln_b4, l, 0, alpha)

        ag, bg, cg, dg = _inproj(x, mod, w_in_r, l)
        y_ml, y_gla = _recurrent(ag, bg, dg, ml_conv, gate_bias, wg_pad, gla_bg3, l)
        qt, k, vt = _mla_proj(cg, dg, tables, gq3, gkv3, wqn_t, wq1_t, wq2_t, wuk_b, wuv_t, l)
        y_mla = _attention(qt, k, vt)
        x = _outproj(x, mod, y_ml, y_gla, y_mla, w_out_b, ln_g4, ln_b4, l, alpha)

        x = _ffn(x, mod, ffn2_wi_b, ffn2_wo_b, ln_g4, ln_b4, l, 2, alpha)
    return x
```

```python
import functools
import math

import jax
import jax.numpy as jnp
import numpy as np
from jax import lax
from jax.experimental import pallas as pl
from jax.experimental.pallas import tpu as pltpu

F32 = jnp.float32
BF16 = jnp.bfloat16

ML_HEADS = 4
ML_DH = 128
ML_W = ML_HEADS * ML_DH
ML_CONV = 4
GLA_HEADS = 4
GLA_DK = 64
GLA_DV = 128
GLA_KW = GLA_HEADS * GLA_DK
GLA_VW = GLA_HEADS * GLA_DV
GLA_RANK = 16
GLA_TAU = 16.0
GLA_CHUNK = 64
MLA_HEADS = 8
MLA_NOPE = 128
MLA_ROPE = 64
MLA_HALF = MLA_ROPE // 2
MLA_DV = 128
MLA_QK = MLA_NOPE + MLA_ROPE
MLA_RANK = 512
ROPE_THETA = 10000.0
NORM_EPS = 1e-5
FFN_RES_WEIGHT = 0.5
QK_SCALE_LOG2 = math.log2(math.e) * MLA_QK ** -0.5

GRP_A = 4 * ML_W
GRP_B = 2 * GLA_KW + 2 * GLA_VW
GRP_C = 2 * MLA_RANK
GRP_D = 128
D_KR = 0
D_MLI = MLA_ROPE
D_MLF = D_MLI + ML_HEADS
D_GLR = D_MLF + ML_HEADS

VMEM_LIMIT_V7X = 58 * 1024 * 1024

FFN_ROWS = 1024
FFN_COLS = 512
REC_BLOCK = 256
ATTN_BLOCK = 1024
PROJ_ROWS = 512
ADA_COLS = 1024
ROPE_ROWS = 1024


def _cparams(sem):
    return pltpu.CompilerParams(dimension_semantics=sem, vmem_limit_bytes=VMEM_LIMIT_V7X)


def _tile(n, pref):
    t = min(n, pref)
    while n % t:
        t //= 2
    return t


def _layer_spec(arr, l):
    shape = (None,) + arr.shape[1:]
    zeros = (0,) * (arr.ndim - 1)
    return pl.BlockSpec(shape, lambda *_: (l,) + zeros)


def _mod_spec(mod, l, k):
    d = mod.shape[-1]
    return pl.BlockSpec((None, 1, None, 3, d), lambda bb, *_: (l, bb, k, 0, 0))


def _ln_spec(ln, l, k):
    return pl.BlockSpec((None, None, 1, ln.shape[-1]), lambda *_: (l, k, 0, 0))


def _sigmoid(x):
    return jax.nn.sigmoid(x)


def _log_sigmoid(x):
    return jnp.minimum(x, 0.0) - jnp.log1p(jnp.exp(-jnp.abs(x)))


def _modulate(x, mod_ref):
    return x * (1.0 + mod_ref[0, 1:2, :]) + mod_ref[0, 0:1, :]


def _residual_layer_norm(x, r, alpha, g_ref, b_ref):
    z = alpha * x + r
    mu = jnp.mean(z, axis=-1, keepdims=True)
    zc = z - mu
    var = jnp.mean(zc * zc, axis=-1, keepdims=True)
    return zc * lax.rsqrt(var + NORM_EPS) * g_ref[...] + b_ref[...]


def _ada_kernel(c_ref, w_ref, b_ref, o_ref):
    c = c_ref[...]
    ca = (c * _sigmoid(c)).astype(BF16)
    o_ref[0] = jnp.dot(ca, w_ref[0].astype(BF16), preferred_element_type=F32) + b_ref[0]


def _ada_mod(c, w_ada, b_ada):
    depth, d, n = w_ada.shape
    bsz = c.shape[0]
    tn = _tile(n, ADA_COLS)
    return pl.pallas_call(
        _ada_kernel,
        out_shape=jax.ShapeDtypeStruct((depth, bsz, n), F32),
        grid=(depth, n // tn),
        in_specs=[
            pl.BlockSpec((bsz, d), lambda l, j: (0, 0)),
            pl.BlockSpec((1, d, tn), lambda l, j: (l, 0, j)),
            pl.BlockSpec((1, 1, tn), lambda l, j: (l, 0, j)),
        ],
        out_specs=pl.BlockSpec((1, bsz, tn), lambda l, j: (l, 0, j)),
        compiler_params=_cparams(("parallel", "parallel")),
        name="ada_mod",
    )(c, w_ada, b_ada.reshape(depth, 1, n))


def _rope_kernel(pcol_ref, prow_ref, frow_ref, fcol_ref, cos_ref, sin_ref, cost_ref, sint_ref):
    ang = pcol_ref[0].astype(F32) * frow_ref[...]
    cos_ref[0] = jnp.cos(ang)
    sin_ref[0] = jnp.sin(ang)
    ang_t = fcol_ref[...] * prow_ref[0].astype(F32)
    cost_ref[0] = jnp.cos(ang_t)
    sint_ref[0] = jnp.sin(ang_t)


def _rope_tables(positions):
    bsz, seq = positions.shape
    ts = _tile(seq, ROPE_ROWS)
    inv_freq = (ROPE_THETA ** (-np.arange(MLA_HALF, dtype=np.float32) / MLA_HALF)).astype(np.float32)
    nat = jax.ShapeDtypeStruct((bsz, seq, MLA_HALF), F32)
    tra = jax.ShapeDtypeStruct((bsz, MLA_HALF, seq), F32)
    return pl.pallas_call(
        _rope_kernel,
        out_shape=(nat, nat, tra, tra),
        grid=(bsz, seq // ts),
        in_specs=[
            pl.BlockSpec((1, ts, 1), lambda b, i: (b, i, 0)),
            pl.BlockSpec((1, 1, ts), lambda b, i: (b, 0, i)),
            pl.BlockSpec((1, MLA_HALF), lambda b, i: (0, 0)),
            pl.BlockSpec((MLA_HALF, 1), lambda b, i: (0, 0)),
        ],
        out_specs=(pl.BlockSpec((1, ts, MLA_HALF), lambda b, i: (b, i, 0)),
                   pl.BlockSpec((1, ts, MLA_HALF), lambda b, i: (b, i, 0)),
                   pl.BlockSpec((1, MLA_HALF, ts), lambda b, i: (b, 0, i)),
                   pl.BlockSpec((1, MLA_HALF, ts), lambda b, i: (b, 0, i))),
        compiler_params=_cparams(("parallel", "parallel")),
        name="rope_tables",
    )(positions.reshape(bsz, seq, 1), positions.reshape(bsz, 1, seq),
      jnp.asarray(inv_freq[None, :]), jnp.asarray(inv_freq[:, None]))


def _ffn_kernel(x_ref, mod_ref, wi_ref, wo_ref, g_ref, b_ref, o_ref, *, alpha, nf):
    j = pl.program_id(2)
    half = x_ref.shape[1] // 2
    gate = FFN_RES_WEIGHT * (1.0 + mod_ref[0, 2:3, :])

    def accumulate(r, first):
        rows = pl.ds(r * half, half)
        u = _modulate(x_ref[0, rows, :], mod_ref).astype(BF16)
        g = jnp.dot(u, wi_ref[0], preferred_element_type=F32)
        up = jnp.dot(u, wi_ref[1], preferred_element_type=F32)
        a = (g * _sigmoid(g) * up).astype(BF16)
        y = jnp.dot(a, wo_ref[...], preferred_element_type=F32)
        if first:
            o_ref[0, rows, :] = y
        else:
            o_ref[0, rows, :] += y

    def finish(r):
        for c in range(2):
            rows = pl.ds(r * half + c * (half // 2), half // 2)
            o_ref[0, rows, :] = _residual_layer_norm(x_ref[0, rows, :], o_ref[0, rows, :] * gate,
                                                     alpha, g_ref, b_ref)

    def step(first, last):
        accumulate(0, first)
        accumulate(1, first)
        if last:
            finish(0)
            finish(1)

    if nf == 1:
        step(True, True)
    else:
        pl.when(j == 0)(lambda: step(True, False))
        if nf > 2:
            pl.when((j > 0) & (j < nf - 1))(lambda: step(False, False))
        pl.when(j == nf - 1)(lambda: step(False, True))


def _ffn(x, mod, wi, wo, ln_g, ln_b, l, k, alpha):
    bsz, seq, d = x.shape
    f = wo.shape[1]
    tm = _tile(seq, FFN_ROWS)
    tf = _tile(f, FFN_COLS)
    nf = f // tf
    return pl.pallas_call(
        functools.partial(_ffn_kernel, alpha=alpha, nf=nf),
        out_shape=jax.ShapeDtypeStruct(x.shape, F32),
        grid=(bsz, seq // tm, nf),
        in_specs=[
            pl.BlockSpec((1, tm, d), lambda bb, i, j: (bb, i, 0)),
            _mod_spec(mod, l, k),
            pl.BlockSpec((None, 2, d, tf), lambda bb, i, j: (l, 0, 0, j)),
            pl.BlockSpec((None, tf, d), lambda bb, i, j: (l, j, 0)),
            _ln_spec(ln_g, l, k),
            _ln_spec(ln_b, l, k),
        ],
        out_specs=pl.BlockSpec((1, tm, d), lambda bb, i, j: (bb, i, 0)),
        compiler_params=_cparams(("parallel", "parallel", "arbitrary")),
        name="ffn",
    )(x, mod, wi, wo, ln_g, ln_b)


def _inproj_kernel(x_ref, mod_ref, w_ref, a_ref, b_ref, c_ref, d_ref):
    u = _modulate(x_ref[0], mod_ref).astype(BF16)
    o0, o1, o2, o3 = 0, GRP_A, GRP_A + GRP_B, GRP_A + GRP_B + GRP_C
    a_ref[0] = jnp.dot(u, w_ref[:, o0:o1], preferred_element_type=F32).astype(BF16)
    b_ref[0] = jnp.dot(u, w_ref[:, o1:o2], preferred_element_type=F32).astype(BF16)
    c_ref[0] = jnp.dot(u, w_ref[:, o2:o3], preferred_element_type=F32).astype(BF16)
    d_ref[0] = jnp.dot(u, w_ref[:, o3:o3 + GRP_D], preferred_element_type=F32)


def _inproj(x, mod, w_r, l):
    bsz, seq, d = x.shape
    tm = _tile(seq, PROJ_ROWS)
    n = w_r.shape[-1]

    def row(w):
        return pl.BlockSpec((1, tm, w), lambda bb, i: (bb, i, 0))

    return pl.pallas_call(
        _inproj_kernel,
        out_shape=(jax.ShapeDtypeStruct((bsz, seq, GRP_A), BF16),
                   jax.ShapeDtypeStruct((bsz, seq, GRP_B), BF16),
                   jax.ShapeDtypeStruct((bsz, seq, GRP_C), BF16),
                   jax.ShapeDtypeStruct((bsz, seq, GRP_D), F32)),
        grid=(bsz, seq // tm),
        in_specs=[
            row(d),
            _mod_spec(mod, l, 1),
            pl.BlockSpec((None, d, n), lambda bb, i: (l, 0, 0), pipeline_mode=pl.Buffered(1)),
        ],
        out_specs=(row(GRP_A), row(GRP_B), row(GRP_C), row(GRP_D)),
        compiler_params=_cparams(("parallel", "parallel")),
        name="inproj",
    )(x, mod, w_r)


ML_PAD = 8


def _mlstm_step(a_ref, d_ref, conv_ref, bias_ref, o_ref, xbuf, c_scr, n_scr, m_scr):
    blk = a_ref.shape[1]
    pad = ML_PAD

    xbuf[pad:pad + blk, :] = a_ref[0, :, 0:2 * ML_W].astype(F32)
    qk = conv_ref[ML_CONV - 1:ML_CONV, :] * xbuf[pad:pad + blk, :]
    for tap in range(ML_CONV - 1):
        sh = ML_CONV - 1 - tap
        qk = qk + conv_ref[tap:tap + 1, :] * xbuf[pad - sh:pad - sh + blk, :]
    qk = qk * _sigmoid(qk)
    xbuf[0:pad, :] = xbuf[blk:blk + pad, :]

    pre = d_ref[0] + bias_ref[...]
    row = lax.broadcasted_iota(jnp.int32, (blk, blk), 0)
    col = lax.broadcasted_iota(jnp.int32, (blk, blk), 1)
    causal = col <= row
    eye = col == row
    tri = jnp.where(causal, 1.0, 0.0).astype(F32)
    bcum = jnp.dot(tri, _log_sigmoid(pre), preferred_element_type=F32,
                   precision=lax.Precision.HIGHEST)

    outs = []
    for h in range(ML_HEADS):
        b_col = bcum[:, D_MLF + h:D_MLF + h + 1]
        r_col = pre[:, D_MLI + h:D_MLI + h + 1] - b_col
        r_row = jnp.sum(jnp.where(eye, r_col, 0.0), axis=0, keepdims=True)
        b_last = b_col[blk - 1:blk, :]
        m_prev = m_scr[h, 0:1, 0:1]
        n_prev = n_scr[h]
        c_prev = c_scr[h]

        q = qk[:, h * ML_DH:(h + 1) * ML_DH]
        k = qk[:, ML_W + h * ML_DH:ML_W + (h + 1) * ML_DH] * (ML_DH ** -0.5)
        v = a_ref[0, :, 2 * ML_W + h * ML_DH:2 * ML_W + (h + 1) * ML_DH]
        og = a_ref[0, :, 3 * ML_W + h * ML_DH:3 * ML_W + (h + 1) * ML_DH].astype(F32)
        qb = q.astype(BF16)

        dmat = jnp.where(causal, b_col + r_row, -jnp.inf)
        m_inter = b_col + m_prev
        m_t = jnp.maximum(m_inter, jnp.max(dmat, axis=1, keepdims=True))
        s = lax.dot_general(qb, k.astype(BF16), (((1,), (1,)), ((), ())),
                            preferred_element_type=F32) * jnp.exp(dmat - m_t)
        inter = jnp.exp(m_inter - m_t)
        num = (jnp.dot(s.astype(BF16), v, preferred_element_type=F32)
               + inter * jnp.dot(qb, c_prev.astype(BF16), preferred_element_type=F32))
        den = (jnp.sum(s, axis=1, keepdims=True)
               + inter * jnp.sum(q * n_prev, axis=1, keepdims=True))
        hh = num / jnp.maximum(jnp.abs(den), jnp.exp(-m_t))
        mu = jnp.mean(hh, axis=-1, keepdims=True)
        hc = hh - mu
        var = jnp.mean(hc * hc, axis=-1, keepdims=True)
        outs.append(_sigmoid(og) * (hc * lax.rsqrt(var + NORM_EPS)))

        a_col = b_last + r_col
        a_max = jnp.max(a_col, axis=0, keepdims=True)
        kw = k * jnp.exp(a_col - a_max)
        c_loc = lax.dot_general(kw.astype(BF16), v, (((0,), (0,)), ((), ())),
                                preferred_element_type=F32)
        n_loc = jnp.sum(kw, axis=0, keepdims=True)
        m_new = jnp.maximum(b_last + m_prev, a_max)
        dec = jnp.exp(b_last + m_prev - m_new)
        inj = jnp.exp(a_max - m_new)
        c_scr[h] = dec * c_prev + inj * c_loc
        n_scr[h] = dec * n_prev + inj * n_loc
        m_scr[h] = jnp.broadcast_to(m_new, m_scr.shape[1:])

    o_ref[0] = jnp.concatenate(outs, axis=-1).astype(o_ref.dtype)


def _gla_step(b_ref, d_ref, wg_ref, bg_ref, o_ref, st_scr):
    blk = b_ref.shape[1]
    ch = GLA_CHUNK

    gate_pre = jnp.dot(d_ref[0].astype(BF16), wg_ref[...],
                       preferred_element_type=F32) + bg_ref[...]
    log_a = _log_sigmoid(gate_pre) / GLA_TAU
    row = lax.broadcasted_iota(jnp.int32, (blk, blk), 0)
    col = lax.broadcasted_iota(jnp.int32, (blk, blk), 1)
    same_chunk = (row // ch) == (col // ch)
    tri = jnp.where((col <= row) & same_chunk, 1.0, 0.0).astype(F32)
    bc = jnp.dot(tri, log_a, preferred_element_type=F32,
                 precision=lax.Precision.HIGHEST)
    e_pos = jnp.exp(bc)
    e_neg = jnp.exp(-bc)
    crow = lax.broadcasted_iota(jnp.int32, (ch, ch), 0)
    ccol = lax.broadcasted_iota(jnp.int32, (ch, ch), 1)
    causal = ccol <= crow

    for sc in range(blk // ch):
        r0, r1 = sc * ch, (sc + 1) * ch
        outs = []
        for h in range(GLA_HEADS):
            k0, k1 = h * GLA_DK, (h + 1) * GLA_DK
            q = b_ref[0, r0:r1, k0:k1].astype(F32) * (GLA_DK ** -0.5)
            k = b_ref[0, r0:r1, GLA_KW + k0:GLA_KW + k1].astype(F32)
            v = b_ref[0, r0:r1, 2 * GLA_KW + h * GLA_DV:2 * GLA_KW + (h + 1) * GLA_DV]
            rg = b_ref[0, r0:r1, 2 * GLA_KW + GLA_VW + h * GLA_DV:
                       2 * GLA_KW + GLA_VW + (h + 1) * GLA_DV].astype(F32)
            bc_h = bc[r0:r1, k0:k1]
            b_last = bc_h[ch - 1:ch, :]
            q_dec = (q * e_pos[r0:r1, k0:k1]).astype(BF16)
            k_inv = (k * e_neg[r0:r1, k0:k1]).astype(BF16)
            attn = lax.dot_general(q_dec, k_inv, (((1,), (1,)), ((), ())),
                                   preferred_element_type=F32)
            attn = jnp.where(causal, attn, 0.0)
            st = st_scr[h]
            o = (jnp.dot(attn.astype(BF16), v, preferred_element_type=F32)
                 + lax.dot_general(q_dec, st.astype(BF16), (((1,), (1,)), ((), ())),
                                   preferred_element_type=F32))
            k_w = (k * jnp.exp(b_last - bc_h)).astype(BF16)
            s_loc_t = lax.dot_general(v, k_w, (((0,), (0,)), ((), ())),
                                      preferred_element_type=F32)
            st_scr[h] = st * jnp.exp(b_last) + s_loc_t
            on = o * lax.rsqrt(jnp.mean(o * o, axis=-1, keepdims=True) + NORM_EPS)
            outs.append(rg * _sigmoid(rg) * on)
        o_ref[0, r0:r1, :] = jnp.concatenate(outs, axis=-1).astype(o_ref.dtype)


def _recurrent_kernel(a_ref, b_ref, d_ref, conv_ref, bias_ref, wg_ref, bg_ref, oml_ref, ogla_ref,
                      xbuf, c_scr, n_scr, m_scr, st_scr):
    @pl.when(pl.program_id(1) == 0)
    def _():
        xbuf[0:ML_PAD, :] = jnp.zeros((ML_PAD, 2 * ML_W), F32)
        c_scr[...] = jnp.zeros_like(c_scr)
        n_scr[...] = jnp.zeros_like(n_scr)
        m_scr[...] = jnp.zeros_like(m_scr)
        st_scr[...] = jnp.zeros_like(st_scr)

    _mlstm_step(a_ref, d_ref, conv_ref, bias_ref, oml_ref, xbuf, c_scr, n_scr, m_scr)
    _gla_step(b_ref, d_ref, wg_ref, bg_ref, ogla_ref, st_scr)


def _recurrent(ag, bg, dg, conv, gate_bias, wg_pad, gla_bg, l):
    bsz, seq, _ = ag.shape
    blk = _tile(seq, REC_BLOCK)

    def row(w):
        return pl.BlockSpec((1, blk, w), lambda bb, c: (bb, c, 0))

    return pl.pallas_call(
        _recurrent_kernel,
        out_shape=(jax.ShapeDtypeStruct((bsz, seq, ML_W), BF16),
                   jax.ShapeDtypeStruct((bsz, seq, GLA_VW), BF16)),
        grid=(bsz, seq // blk),
        in_specs=[row(GRP_A), row(GRP_B), row(GRP_D),
                  _layer_spec(conv, l), _layer_spec(gate_bias, l),
                  _layer_spec(wg_pad, l), _layer_spec(gla_bg, l)],
        out_specs=(row(ML_W), row(GLA_VW)),
        scratch_shapes=[
            pltpu.VMEM((blk + ML_PAD, 2 * ML_W), F32),
            pltpu.VMEM((ML_HEADS, ML_DH, ML_DH), F32),
            pltpu.VMEM((ML_HEADS, 1, ML_DH), F32),
            pltpu.VMEM((ML_HEADS, 1, ML_DH), F32),
            pltpu.VMEM((GLA_HEADS, GLA_DV, GLA_DK), F32),
        ],
        compiler_params=_cparams(("parallel", "arbitrary")),
        name="recurrent",
    )(ag, bg, dg, conv, gate_bias, wg_pad, gla_bg)


def _rms(x, g_ref):
    return x * lax.rsqrt(jnp.mean(x * x, axis=-1, keepdims=True) + NORM_EPS) * g_ref[...]


def _dot_nt(a, b):
    return lax.dot_general(a, b, (((1,), (1,)), ((), ())), preferred_element_type=F32)


def _mla_proj_kernel(c_ref, d_ref, cos_ref, sin_ref, cost_ref, sint_ref, gq_ref, gkv_ref,
                     wqn_ref, wq1_ref, wq2_ref, wuk_ref, wuv_ref, qt_ref, k_ref, vt_ref):
    cq = c_ref[0, :, 0:MLA_RANK].astype(F32)
    ckv = c_ref[0, :, MLA_RANK:2 * MLA_RANK].astype(F32)
    qn = _rms(cq, gq_ref).astype(BF16)
    kvn = _rms(ckv, gkv_ref).astype(BF16)

    q_nope_t = _dot_nt(wqn_ref[...], qn)
    q1_t = _dot_nt(wq1_ref[...], qn)
    q2_t = _dot_nt(wq2_ref[...], qn)
    cos_t = jnp.concatenate([cost_ref[0]] * MLA_HEADS, axis=0)
    sin_t = jnp.concatenate([sint_ref[0]] * MLA_HEADS, axis=0)
    qr1_t = q1_t * cos_t - q2_t * sin_t
    qr2_t = q1_t * sin_t + q2_t * cos_t

    k_nope = jnp.dot(kvn, wuk_ref[...], preferred_element_type=F32)
    v_t = _dot_nt(wuv_ref[...], kvn)
    kr = d_ref[0]
    k1 = kr[:, D_KR:D_KR + MLA_HALF]
    k2 = kr[:, D_KR + MLA_HALF:D_KR + MLA_ROPE]
    cos = cos_ref[0]
    sin = sin_ref[0]
    kr1 = (k1 * cos - k2 * sin).astype(BF16)
    kr2 = (k1 * sin + k2 * cos).astype(BF16)
    for h in range(MLA_HEADS):
        n0, n1 = h * MLA_NOPE, (h + 1) * MLA_NOPE
        h0, h1 = h * MLA_HALF, (h + 1) * MLA_HALF
        q_t = jnp.concatenate([q_nope_t[n0:n1], qr1_t[h0:h1], qr2_t[h0:h1]], axis=0)
        qt_ref[0, h] = (q_t * QK_SCALE_LOG2).astype(BF16)
        k_ref[0, h] = jnp.concatenate([k_nope[:, n0:n1].astype(BF16), kr1, kr2], axis=-1)
        vt_ref[0, h] = v_t[h * MLA_DV:(h + 1) * MLA_DV].astype(BF16)


def _mla_proj(cg, dg, tables, gq, gkv, wqn_t, wq1_t, wq2_t, wuk, wuv_t, l):
    bsz, seq, _ = cg.shape
    tm = _tile(seq, PROJ_ROWS)
    cos, sin, cos_t, sin_t = tables

    def row(w):
        return pl.BlockSpec((1, tm, w), lambda bb, i: (bb, i, 0))

    def col(w):
        return pl.BlockSpec((1, w, tm), lambda bb, i: (bb, 0, i))

    return pl.pallas_call(
        _mla_proj_kernel,
        out_shape=(jax.ShapeDtypeStruct((bsz, MLA_HEADS, MLA_QK, seq), BF16),
                   jax.ShapeDtypeStruct((bsz, MLA_HEADS, seq, MLA_QK), BF16),
                   jax.ShapeDtypeStruct((bsz, MLA_HEADS, MLA_DV, seq), BF16)),
        grid=(bsz, seq // tm),
        in_specs=[row(GRP_C), row(GRP_D), row(MLA_HALF), row(MLA_HALF),
                  col(MLA_HALF), col(MLA_HALF),
                  _layer_spec(gq, l), _layer_spec(gkv, l),
                  _layer_spec(wqn_t, l), _layer_spec(wq1_t, l), _layer_spec(wq2_t, l),
                  _layer_spec(wuk, l), _layer_spec(wuv_t, l)],
        out_specs=(pl.BlockSpec((1, MLA_HEADS, MLA_QK, tm), lambda bb, i: (bb, 0, 0, i)),
                   pl.BlockSpec((1, MLA_HEADS, tm, MLA_QK), lambda bb, i: (bb, 0, i, 0)),
                   pl.BlockSpec((1, MLA_HEADS, MLA_DV, tm), lambda bb, i: (bb, 0, 0, i))),
        compiler_params=_cparams(("parallel", "parallel")),
        name="mla_proj",
    )(cg, dg, cos, sin, cos_t, sin_t, gq, gkv, wqn_t, wq1_t, wq2_t, wuk, wuv_t)


def _attn_kernel(qt_ref, k_ref, vt_ref, o_ref, s_ref, m_ref, l_ref, acc_ref, *, tq):
    seq = k_ref.shape[2]
    nq = seq // tq
    key = lax.broadcasted_iota(jnp.int32, (tq, tq), 0)
    qry = lax.broadcasted_iota(jnp.int32, (tq, tq), 1)
    tri = key <= qry
    tasks = [(i, j) for i in range(nq) for j in range(i + 1)]

    def scores(t):
        i, j = tasks[t]
        qt = qt_ref[0, 0, :, i * tq:(i + 1) * tq]
        k = k_ref[0, 0, j * tq:(j + 1) * tq, :]
        s_ref[t % 2] = jnp.dot(k, qt, preferred_element_type=F32)

    def update(t):
        i, j = tasks[t]
        slot = i % 2
        vt = vt_ref[0, 0, :, j * tq:(j + 1) * tq]
        st = s_ref[t % 2]
        if j == i:
            st = jnp.where(tri, st, -jnp.inf)
        if j == 0:
            m_new = jnp.max(st, axis=0, keepdims=True)
            pt = jnp.exp2(st - m_new)
            l_new = jnp.sum(pt, axis=0, keepdims=True)
            acc_new = jnp.dot(vt, pt.astype(BF16), preferred_element_type=F32)
        else:
            m = m_ref[slot]
            m_new = jnp.maximum(m, jnp.max(st, axis=0, keepdims=True))
            pt = jnp.exp2(st - m_new)
            corr = jnp.exp2(m - m_new)
            l_new = corr * l_ref[slot] + jnp.sum(pt, axis=0, keepdims=True)
            acc_new = corr * acc_ref[slot] + jnp.dot(vt, pt.astype(BF16),
                                                     preferred_element_type=F32)
        if j == i:
            o_ref[0, i * tq:(i + 1) * tq, :] = (acc_new / l_new).T.astype(o_ref.dtype)
        else:
            m_ref[slot] = m_new
            l_ref[slot] = l_new
            acc_ref[slot] = acc_new

    scores(0)
    for t in range(len(tasks)):
        if t + 1 < len(tasks):
            scores(t + 1)
        update(t)


def _attention(qt, k, vt):
    bsz, nh, seq, _ = k.shape
    tq = _tile(seq, ATTN_BLOCK)
    return pl.pallas_call(
        functools.partial(_attn_kernel, tq=tq),
        out_shape=jax.ShapeDtypeStruct((bsz, seq, nh * MLA_DV), BF16),
        grid=(bsz, nh),
        in_specs=[
            pl.BlockSpec((1, 1, MLA_QK, seq), lambda bb, h: (bb, h, 0, 0)),
            pl.BlockSpec((1, 1, seq, MLA_QK), lambda bb, h: (bb, h, 0, 0)),
            pl.BlockSpec((1, 1, MLA_DV, seq), lambda bb, h: (bb, h, 0, 0)),
        ],
        out_specs=pl.BlockSpec((1, seq, MLA_DV), lambda bb, h: (bb, 0, h)),
        scratch_shapes=[pltpu.VMEM((2, tq, tq), F32), pltpu.VMEM((2, 1, tq), F32),
                        pltpu.VMEM((2, 1, tq), F32), pltpu.VMEM((2, MLA_DV, tq), F32)],
        compiler_params=_cparams(("parallel", "parallel")),
        name="mla_attn",
    )(qt, k, vt)


def _outproj_kernel(x_ref, mod_ref, yml_ref, ygla_ref, ymla_ref, w_ref, g_ref, b_ref, o_ref,
                    *, alpha):
    gate = 1.0 + mod_ref[0, 2:3, :]
    half = x_ref.shape[1] // 2
    for r in range(2):
        rows = pl.ds(r * half, half)
        y = jnp.concatenate([yml_ref[0, rows, :], ygla_ref[0, rows, :], ymla_ref[0, rows, :]],
                            axis=-1)
        res = jnp.dot(y, w_ref[...], preferred_element_type=F32) * gate
        o_ref[0, rows, :] = _residual_layer_norm(x_ref[0, rows, :], res, alpha, g_ref, b_ref)


def _outproj(x, mod, y_ml, y_gla, y_mla, w_out, ln_g, ln_b, l, alpha):
    bsz, seq, d = x.shape
    tm = _tile(seq, PROJ_ROWS)

    def row(w):
        return pl.BlockSpec((1, tm, w), lambda bb, i: (bb, i, 0))

    return pl.pallas_call(
        functools.partial(_outproj_kernel, alpha=alpha),
        out_shape=jax.ShapeDtypeStruct(x.shape, F32),
        grid=(bsz, seq // tm),
        in_specs=[
            row(d),
            _mod_spec(mod, l, 1),
            row(y_ml.shape[-1]), row(y_gla.shape[-1]), row(y_mla.shape[-1]),
            pl.BlockSpec((None,) + w_out.shape[1:], lambda bb, i: (l, 0, 0),
                         pipeline_mode=pl.Buffered(1)),
            _ln_spec(ln_g, l, 1),
            _ln_spec(ln_b, l, 1),
        ],
        out_specs=row(d),
        compiler_params=_cparams(("parallel", "parallel")),
        name="outproj",
    )(x, mod, y_ml, y_gla, y_mla, w_out, ln_g, ln_b)


def _regroup_w_in(w_in):
    sizes = (ML_W, ML_W, ML_W, ML_HEADS, ML_HEADS, ML_W,
             GLA_KW, GLA_KW, GLA_VW, GLA_RANK, GLA_VW,
             MLA_RANK, MLA_RANK, MLA_ROPE)
    offs = np.concatenate([[0], np.cumsum(sizes)])
    (ml_q, ml_k, ml_v, ml_i, ml_f, ml_o, gl_q, gl_k, gl_v, gl_lr, gl_r, c_q, c_kv, k_r) = (
        w_in[..., int(offs[n]):int(offs[n + 1])] for n in range(len(sizes)))
    used = MLA_ROPE + 2 * ML_HEADS + GLA_RANK
    pad = jnp.zeros(w_in.shape[:-1] + (GRP_D - used,), w_in.dtype)
    return jnp.concatenate([ml_q, ml_k, ml_v, ml_o, gl_q, gl_k, gl_v, gl_r, c_q, c_kv,
                            k_r, ml_i, ml_f, gl_lr, pad], axis=-1).astype(BF16)


def kernel(x, c, positions, w_ada, b_ada, ln_g, ln_b, ffn1_wi, ffn1_wo, ffn2_wi, ffn2_wo,
           w_in, ml_conv, ml_bi, ml_bf, gla_wg, gla_bg, mla_gq, mla_wuq, mla_gkv,
           mla_wuk, mla_wuv, w_out):
    depth = w_ada.shape[0]
    bsz, seq, d = x.shape
    alpha = (2.0 * depth) ** 0.25

    mod = _ada_mod(c, w_ada, b_ada).reshape(depth, bsz, 3, 3, d)
    tables = _rope_tables(positions)

    w_in_r = _regroup_w_in(w_in)
    def gate_up(wi):
        return jnp.swapaxes(wi.reshape(depth, d, 2, -1), 1, 2).astype(BF16)

    ffn1_wi_b, ffn1_wo_b = gate_up(ffn1_wi), ffn1_wo.astype(BF16)
    ffn2_wi_b, ffn2_wo_b = gate_up(ffn2_wi), ffn2_wo.astype(BF16)
    w_out_b = w_out.astype(BF16)
    ln_g4 = ln_g.reshape(depth, 3, 1, d)
    ln_b4 = ln_b.reshape(depth, 3, 1, d)
    wuq = mla_wuq.reshape(depth, MLA_RANK, MLA_HEADS, MLA_QK)

    def feat_major(w):
        return jnp.swapaxes(w.reshape(depth, MLA_RANK, -1), 1, 2).astype(BF16)

    wqn_t = feat_major(wuq[..., :MLA_NOPE])
    wq1_t = feat_major(wuq[..., MLA_NOPE:MLA_NOPE + MLA_HALF])
    wq2_t = feat_major(wuq[..., MLA_NOPE + MLA_HALF:])
    wuk_b = mla_wuk.astype(BF16)
    wuv_t = feat_major(mla_wuv)
    gq3 = mla_gq.reshape(depth, 1, MLA_RANK)
    gkv3 = mla_gkv.reshape(depth, 1, MLA_RANK)
    gate_bias = jnp.zeros((depth, 1, GRP_D), F32)
    gate_bias = gate_bias.at[:, 0, D_MLI:D_MLI + ML_HEADS].set(ml_bi)
    gate_bias = gate_bias.at[:, 0, D_MLF:D_MLF + ML_HEADS].set(ml_bf)
    wg_pad = jnp.zeros((depth, GRP_D, GLA_KW), F32)
    wg_pad = wg_pad.at[:, D_GLR:D_GLR + GLA_RANK, :].set(gla_wg).astype(BF16)
    gla_bg3 = gla_bg.reshape(depth, 1, GLA_KW)

    for l in range(depth):
        x = _ffn(x, mod, ffn1_wi_b, ffn1_wo_b, ln_g4, ln_b4, l, 0, alpha)

        ag, bg, cg, dg = _inproj(x, mod, w_in_r, l)
        y_ml, y_gla = _recurrent(ag, bg, dg, ml_conv, gate_bias, wg_pad, gla_bg3, l)
        qt, k, vt = _mla_proj(cg, dg, tables, gq3, gkv3, wqn_t, wq1_t, wq2_t, wuk_b, wuv_t, l)
        y_mla = _attention(qt, k, vt)
        x = _outproj(x, mod, y_ml, y_gla, y_mla, w_out_b, ln_g4, ln_b4, l, alpha)

        x = _ffn(x, mod, ffn2_wi_b, ffn2_wo_b, ln_g4, ln_b4, l, 2, alpha)
    return x
```

```python
import functools
import math

import jax
import jax.numpy as jnp
import numpy as np
from jax import lax
from jax.experimental import pallas as pl
from jax.experimental.pallas import tpu as pltpu

F32 = jnp.float32
BF16 = jnp.bfloat16

ML_HEADS = 4
ML_DH = 128
ML_W = ML_HEADS * ML_DH
ML_CONV = 4
GLA_HEADS = 4
GLA_DK = 64
GLA_DV = 128
GLA_KW = GLA_HEADS * GLA_DK
GLA_VW = GLA_HEADS * GLA_DV
GLA_RANK = 16
GLA_TAU = 16.0
GLA_CHUNK = 64
MLA_HEADS = 8
MLA_NOPE = 128
MLA_ROPE = 64
MLA_HALF = MLA_ROPE // 2
MLA_DV = 128
MLA_QK = MLA_NOPE + MLA_ROPE
MLA_RANK = 512
ROPE_THETA = 10000.0
NORM_EPS = 1e-5
FFN_RES_WEIGHT = 0.5
QK_SCALE_LOG2 = math.log2(math.e) * MLA_QK ** -0.5

GRP_A = 4 * ML_W
GRP_B = 2 * GLA_KW + 2 * GLA_VW
GRP_C = 2 * MLA_RANK
GRP_D = 128
D_KR = 0
D_MLI = MLA_ROPE
D_MLF = D_MLI + ML_HEADS
D_GLR = D_MLF + ML_HEADS

VMEM_LIMIT_V7X = 58 * 1024 * 1024

FFN_ROWS = 1024
FFN_COLS = 512
REC_BLOCK = 256
ATTN_BLOCK = 1024
PROJ_ROWS = 512
ADA_COLS = 1024
ROPE_ROWS = 1024
CAST_COLS = 1408


def _cparams(sem):
    return pltpu.CompilerParams(dimension_semantics=sem, vmem_limit_bytes=VMEM_LIMIT_V7X)


def _tile(n, pref):
    t = min(n, pref)
    while n % t:
        t //= 2
    return t


def _layer_spec(arr, l):
    shape = (None,) + arr.shape[1:]
    zeros = (0,) * (arr.ndim - 1)
    return pl.BlockSpec(shape, lambda *_: (l,) + zeros)


def _mod_spec(mod, l, k):
    d = mod.shape[-1]
    return pl.BlockSpec((None, 1, None, 3, d), lambda bb, *_: (l, bb, k, 0, 0))


def _ln_spec(ln, l, k):
    return pl.BlockSpec((None, None, 1, ln.shape[-1]), lambda *_: (l, k, 0, 0))


def _sigmoid(x):
    return jax.nn.sigmoid(x)


def _log_sigmoid(x):
    return jnp.minimum(x, 0.0) - jnp.log1p(jnp.exp(-jnp.abs(x)))


def _modulate(x, mod_ref):
    return x * (1.0 + mod_ref[0, 1:2, :]) + mod_ref[0, 0:1, :]


def _residual_layer_norm(x, r, alpha, g_ref, b_ref):
    z = alpha * x + r
    mu = jnp.mean(z, axis=-1, keepdims=True)
    zc = z - mu
    var = jnp.mean(zc * zc, axis=-1, keepdims=True)
    return zc * lax.rsqrt(var + NORM_EPS) * g_ref[...] + b_ref[...]


def _ada_kernel(c_ref, w_ref, b_ref, o_ref):
    c = c_ref[...]
    ca = (c * _sigmoid(c)).astype(BF16)
    o_ref[0] = jnp.dot(ca, w_ref[0].astype(BF16), preferred_element_type=F32) + b_ref[0]


def _ada_mod(c, w_ada, b_ada):
    depth, d, n = w_ada.shape
    bsz = c.shape[0]
    tn = _tile(n, ADA_COLS)
    return pl.pallas_call(
        _ada_kernel,
        out_shape=jax.ShapeDtypeStruct((depth, bsz, n), F32),
        grid=(depth, n // tn),
        in_specs=[
            pl.BlockSpec((bsz, d), lambda l, j: (0, 0)),
            pl.BlockSpec((1, d, tn), lambda l, j: (l, 0, j)),
            pl.BlockSpec((1, 1, tn), lambda l, j: (l, 0, j)),
        ],
        out_specs=pl.BlockSpec((1, bsz, tn), lambda l, j: (l, 0, j)),
        compiler_params=_cparams(("parallel", "parallel")),
        name="ada_mod",
    )(c, w_ada, b_ada.reshape(depth, 1, n))


def _rope_kernel(pcol_ref, prow_ref, frow_ref, fcol_ref, cos_ref, sin_ref, cost_ref, sint_ref):
    ang = pcol_ref[0].astype(F32) * frow_ref[...]
    cos_ref[0] = jnp.cos(ang)
    sin_ref[0] = jnp.sin(ang)
    ang_t = fcol_ref[...] * prow_ref[0].astype(F32)
    cost_ref[0] = jnp.cos(ang_t)
    sint_ref[0] = jnp.sin(ang_t)


def _rope_tables(positions):
    bsz, seq = positions.shape
    ts = _tile(seq, ROPE_ROWS)
    inv_freq = (ROPE_THETA ** (-np.arange(MLA_HALF, dtype=np.float32) / MLA_HALF)).astype(np.float32)
    nat = jax.ShapeDtypeStruct((bsz, seq, MLA_HALF), F32)
    tra = jax.ShapeDtypeStruct((bsz, MLA_HALF, seq), F32)
    return pl.pallas_call(
        _rope_kernel,
        out_shape=(nat, nat, tra, tra),
        grid=(bsz, seq // ts),
        in_specs=[
            pl.BlockSpec((1, ts, 1), lambda b, i: (b, i, 0)),
            pl.BlockSpec((1, 1, ts), lambda b, i: (b, 0, i)),
            pl.BlockSpec((1, MLA_HALF), lambda b, i: (0, 0)),
            pl.BlockSpec((MLA_HALF, 1), lambda b, i: (0, 0)),
        ],
        out_specs=(pl.BlockSpec((1, ts, MLA_HALF), lambda b, i: (b, i, 0)),
                   pl.BlockSpec((1, ts, MLA_HALF), lambda b, i: (b, i, 0)),
                   pl.BlockSpec((1, MLA_HALF, ts), lambda b, i: (b, 0, i)),
                   pl.BlockSpec((1, MLA_HALF, ts), lambda b, i: (b, 0, i))),
        compiler_params=_cparams(("parallel", "parallel")),
        name="rope_tables",
    )(positions.reshape(bsz, seq, 1), positions.reshape(bsz, 1, seq),
      jnp.asarray(inv_freq[None, :]), jnp.asarray(inv_freq[:, None]))


def _ffn_kernel(x_ref, mod_ref, wi_ref, wo_ref, g_ref, b_ref, o_ref, *, alpha, nf):
    j = pl.program_id(2)
    half = x_ref.shape[1] // 2
    gate = FFN_RES_WEIGHT * (1.0 + mod_ref[0, 2:3, :])

    def accumulate(r, first):
        rows = pl.ds(r * half, half)
        u = _modulate(x_ref[0, rows, :], mod_ref).astype(BF16)
        g = jnp.dot(u, wi_ref[0], preferred_element_type=F32)
        up = jnp.dot(u, wi_ref[1], preferred_element_type=F32)
        a = (g * _sigmoid(g) * up).astype(BF16)
        y = jnp.dot(a, wo_ref[...], preferred_element_type=F32)
        if first:
            o_ref[0, rows, :] = y
        else:
            o_ref[0, rows, :] += y

    def finish(r):
        for c in range(2):
            rows = pl.ds(r * half + c * (half // 2), half // 2)
            o_ref[0, rows, :] = _residual_layer_norm(x_ref[0, rows, :], o_ref[0, rows, :] * gate,
                                                     alpha, g_ref, b_ref)

    def step(first, last):
        accumulate(0, first)
        accumulate(1, first)
        if last:
            finish(0)
            finish(1)

    if nf == 1:
        step(True, True)
    else:
        pl.when(j == 0)(lambda: step(True, False))
        if nf > 2:
            pl.when((j > 0) & (j < nf - 1))(lambda: step(False, False))
        pl.when(j == nf - 1)(lambda: step(False, True))


def _ffn(x, mod, wi, wo, ln_g, ln_b, l, k, alpha):
    bsz, seq, d = x.shape
    f = wo.shape[1]
    tm = _tile(seq, FFN_ROWS)
    tf = _tile(f, FFN_COLS)
    nf = f // tf
    return pl.pallas_call(
        functools.partial(_ffn_kernel, alpha=alpha, nf=nf),
        out_shape=jax.ShapeDtypeStruct(x.shape, F32),
        grid=(bsz, seq // tm, nf),
        in_specs=[
            pl.BlockSpec((1, tm, d), lambda bb, i, j: (bb, i, 0)),
            _mod_spec(mod, l, k),
            pl.BlockSpec((None, 2, d, tf), lambda bb, i, j: (l, 0, 0, j)),
            pl.BlockSpec((None, tf, d), lambda bb, i, j: (l, j, 0)),
            _ln_spec(ln_g, l, k),
            _ln_spec(ln_b, l, k),
        ],
        out_specs=pl.BlockSpec((1, tm, d), lambda bb, i, j: (bb, i, 0)),
        compiler_params=_cparams(("parallel", "parallel", "arbitrary")),
        name="ffn",
    )(x, mod, wi, wo, ln_g, ln_b)


def _inproj_kernel(x_ref, mod_ref, w_ref, a_ref, b_ref, c_ref, d_ref):
    u = _modulate(x_ref[0], mod_ref).astype(BF16)
    o0, o1, o2, o3 = 0, GRP_A, GRP_A + GRP_B, GRP_A + GRP_B + GRP_C
    a_ref[0] = jnp.dot(u, w_ref[:, o0:o1], preferred_element_type=F32).astype(BF16)
    b_ref[0] = jnp.dot(u, w_ref[:, o1:o2], preferred_element_type=F32).astype(BF16)
    c_ref[0] = jnp.dot(u, w_ref[:, o2:o3], preferred_element_type=F32).astype(BF16)
    d_ref[0] = jnp.dot(u, w_ref[:, o3:o3 + GRP_D], preferred_element_type=F32)


def _inproj(x, mod, w_r, l):
    bsz, seq, d = x.shape
    tm = _tile(seq, PROJ_ROWS)
    n = w_r.shape[-1]

    def row(w):
        return pl.BlockSpec((1, tm, w), lambda bb, i: (bb, i, 0))

    return pl.pallas_call(
        _inproj_kernel,
        out_shape=(jax.ShapeDtypeStruct((bsz, seq, GRP_A), BF16),
                   jax.ShapeDtypeStruct((bsz, seq, GRP_B), BF16),
                   jax.ShapeDtypeStruct((bsz, seq, GRP_C), BF16),
                   jax.ShapeDtypeStruct((bsz, seq, GRP_D), F32)),
        grid=(bsz, seq // tm),
        in_specs=[
            row(d),
            _mod_spec(mod, l, 1),
            pl.BlockSpec((None, d, n), lambda bb, i: (l, 0, 0), pipeline_mode=pl.Buffered(1)),
        ],
        out_specs=(row(GRP_A), row(GRP_B), row(GRP_C), row(GRP_D)),
        compiler_params=_cparams(("parallel", "parallel")),
        name="inproj",
    )(x, mod, w_r)


ML_PAD = 8


def _mlstm_step(a_ref, d_ref, conv_ref, bias_ref, o_ref, xbuf, c_scr, n_scr, m_scr):
    blk = a_ref.shape[1]
    pad = ML_PAD

    xbuf[pad:pad + blk, :] = a_ref[0, :, 0:2 * ML_W].astype(F32)
    qk = conv_ref[ML_CONV - 1:ML_CONV, :] * xbuf[pad:pad + blk, :]
    for tap in range(ML_CONV - 1):
        sh = ML_CONV - 1 - tap
        qk = qk + conv_ref[tap:tap + 1, :] * xbuf[pad - sh:pad - sh + blk, :]
    qk = qk * _sigmoid(qk)
    xbuf[0:pad, :] = xbuf[blk:blk + pad, :]

    pre = d_ref[0] + bias_ref[...]
    row = lax.broadcasted_iota(jnp.int32, (blk, blk), 0)
    col = lax.broadcasted_iota(jnp.int32, (blk, blk), 1)
    causal = col <= row
    eye = col == row
    tri = jnp.where(causal, 1.0, 0.0).astype(F32)
    bcum = jnp.dot(tri, _log_sigmoid(pre), preferred_element_type=F32,
                   precision=lax.Precision.HIGHEST)

    outs = []
    for h in range(ML_HEADS):
        b_col = bcum[:, D_MLF + h:D_MLF + h + 1]
        r_col = pre[:, D_MLI + h:D_MLI + h + 1] - b_col
        r_row = jnp.sum(jnp.where(eye, r_col, 0.0), axis=0, keepdims=True)
        b_last = b_col[blk - 1:blk, :]
        m_prev = m_scr[h, 0:1, 0:1]
        n_prev = n_scr[h]
        c_prev = c_scr[h]

        q = qk[:, h * ML_DH:(h + 1) * ML_DH]
        k = qk[:, ML_W + h * ML_DH:ML_W + (h + 1) * ML_DH] * (ML_DH ** -0.5)
        v = a_ref[0, :, 2 * ML_W + h * ML_DH:2 * ML_W + (h + 1) * ML_DH]
        og = a_ref[0, :, 3 * ML_W + h * ML_DH:3 * ML_W + (h + 1) * ML_DH].astype(F32)
        qb = q.astype(BF16)

        dmat = jnp.where(causal, b_col + r_row, -jnp.inf)
        m_inter = b_col + m_prev
        m_t = jnp.maximum(m_inter, jnp.max(dmat, axis=1, keepdims=True))
        s = lax.dot_general(qb, k.astype(BF16), (((1,), (1,)), ((), ())),
                            preferred_element_type=F32) * jnp.exp(dmat - m_t)
        inter = jnp.exp(m_inter - m_t)
        num = (jnp.dot(s.astype(BF16), v, preferred_element_type=F32)
               + inter * jnp.dot(qb, c_prev.astype(BF16), preferred_element_type=F32))
        den = (jnp.sum(s, axis=1, keepdims=True)
               + inter * jnp.sum(q * n_prev, axis=1, keepdims=True))
        hh = num / jnp.maximum(jnp.abs(den), jnp.exp(-m_t))
        mu = jnp.mean(hh, axis=-1, keepdims=True)
        hc = hh - mu
        var = jnp.mean(hc * hc, axis=-1, keepdims=True)
        outs.append(_sigmoid(og) * (hc * lax.rsqrt(var + NORM_EPS)))

        a_col = b_last + r_col
        a_max = jnp.max(a_col, axis=0, keepdims=True)
        kw = k * jnp.exp(a_col - a_max)
        c_loc = lax.dot_general(kw.astype(BF16), v, (((0,), (0,)), ((), ())),
                                preferred_element_type=F32)
        n_loc = jnp.sum(kw, axis=0, keepdims=True)
        m_new = jnp.maximum(b_last + m_prev, a_max)
        dec = jnp.exp(b_last + m_prev - m_new)
        inj = jnp.exp(a_max - m_new)
        c_scr[h] = dec * c_prev + inj * c_loc
        n_scr[h] = dec * n_prev + inj * n_loc
        m_scr[h] = jnp.broadcast_to(m_new, m_scr.shape[1:])

    o_ref[0] = jnp.concatenate(outs, axis=-1).astype(o_ref.dtype)


def _gla_step(b_ref, d_ref, wg_ref, bg_ref, o_ref, st_scr):
    blk = b_ref.shape[1]
    ch = GLA_CHUNK

    gate_pre = jnp.dot(d_ref[0].astype(BF16), wg_ref[...],
                       preferred_element_type=F32) + bg_ref[...]
    log_a = _log_sigmoid(gate_pre) / GLA_TAU
    row = lax.broadcasted_iota(jnp.int32, (blk, blk), 0)
    col = lax.broadcasted_iota(jnp.int32, (blk, blk), 1)
    same_chunk = (row // ch) == (col // ch)
    tri = jnp.where((col <= row) & same_chunk, 1.0, 0.0).astype(F32)
    bc = jnp.dot(tri, log_a, preferred_element_type=F32,
                 precision=lax.Precision.HIGHEST)
    e_pos = jnp.exp(bc)
    e_neg = jnp.exp(-bc)
    crow = lax.broadcasted_iota(jnp.int32, (ch, ch), 0)
    ccol = lax.broadcasted_iota(jnp.int32, (ch, ch), 1)
    causal = ccol <= crow

    for sc in range(blk // ch):
        r0, r1 = sc * ch, (sc + 1) * ch
        outs = []
        for h in range(GLA_HEADS):
            k0, k1 = h * GLA_DK, (h + 1) * GLA_DK
            q = b_ref[0, r0:r1, k0:k1].astype(F32) * (GLA_DK ** -0.5)
            k = b_ref[0, r0:r1, GLA_KW + k0:GLA_KW + k1].astype(F32)
            v = b_ref[0, r0:r1, 2 * GLA_KW + h * GLA_DV:2 * GLA_KW + (h + 1) * GLA_DV]
            rg = b_ref[0, r0:r1, 2 * GLA_KW + GLA_VW + h * GLA_DV:
                       2 * GLA_KW + GLA_VW + (h + 1) * GLA_DV].astype(F32)
            bc_h = bc[r0:r1, k0:k1]
            b_last = bc_h[ch - 1:ch, :]
            q_dec = (q * e_pos[r0:r1, k0:k1]).astype(BF16)
            k_inv = (k * e_neg[r0:r1, k0:k1]).astype(BF16)
            attn = lax.dot_general(q_dec, k_inv, (((1,), (1,)), ((), ())),
                                   preferred_element_type=F32)
            attn = jnp.where(causal, attn, 0.0)
            st = st_scr[h]
            o = (jnp.dot(attn.astype(BF16), v, preferred_element_type=F32)
                 + lax.dot_general(q_dec, st.astype(BF16), (((1,), (1,)), ((), ())),
                                   preferred_element_type=F32))
            k_w = (k * jnp.exp(b_last - bc_h)).astype(BF16)
            s_loc_t = lax.dot_general(v, k_w, (((0,), (0,)), ((), ())),
                                      preferred_element_type=F32)
            st_scr[h] = st * jnp.exp(b_last) + s_loc_t
            on = o * lax.rsqrt(jnp.mean(o * o, axis=-1, keepdims=True) + NORM_EPS)
            outs.append(rg * _sigmoid(rg) * on)
        o_ref[0, r0:r1, :] = jnp.concatenate(outs, axis=-1).astype(o_ref.dtype)


def _recurrent_kernel(a_ref, b_ref, d_ref, conv_ref, bias_ref, wg_ref, bg_ref, oml_ref, ogla_ref,
                      xbuf, c_scr, n_scr, m_scr, st_scr):
    @pl.when(pl.program_id(1) == 0)
    def _():
        xbuf[0:ML_PAD, :] = jnp.zeros((ML_PAD, 2 * ML_W), F32)
        c_scr[...] = jnp.zeros_like(c_scr)
        n_scr[...] = jnp.zeros_like(n_scr)
        m_scr[...] = jnp.zeros_like(m_scr)
        st_scr[...] = jnp.zeros_like(st_scr)

    _mlstm_step(a_ref, d_ref, conv_ref, bias_ref, oml_ref, xbuf, c_scr, n_scr, m_scr)
    _gla_step(b_ref, d_ref, wg_ref, bg_ref, ogla_ref, st_scr)


def _recurrent(ag, bg, dg, conv, gate_bias, wg_pad, gla_bg, l):
    bsz, seq, _ = ag.shape
    blk = _tile(seq, REC_BLOCK)

    def row(w):
        return pl.BlockSpec((1, blk, w), lambda bb, c: (bb, c, 0))

    return pl.pallas_call(
        _recurrent_kernel,
        out_shape=(jax.ShapeDtypeStruct((bsz, seq, ML_W), BF16),
                   jax.ShapeDtypeStruct((bsz, seq, GLA_VW), BF16)),
        grid=(bsz, seq // blk),
        in_specs=[row(GRP_A), row(GRP_B), row(GRP_D),
                  _layer_spec(conv, l), _layer_spec(gate_bias, l),
                  _layer_spec(wg_pad, l), _layer_spec(gla_bg, l)],
        out_specs=(row(ML_W), row(GLA_VW)),
        scratch_shapes=[
            pltpu.VMEM((blk + ML_PAD, 2 * ML_W), F32),
            pltpu.VMEM((ML_HEADS, ML_DH, ML_DH), F32),
            pltpu.VMEM((ML_HEADS, 1, ML_DH), F32),
            pltpu.VMEM((ML_HEADS, 1, ML_DH), F32),
            pltpu.VMEM((GLA_HEADS, GLA_DV, GLA_DK), F32),
        ],
        compiler_params=_cparams(("parallel", "arbitrary")),
        name="recurrent",
    )(ag, bg, dg, conv, gate_bias, wg_pad, gla_bg)


def _rms(x, g_ref):
    return x * lax.rsqrt(jnp.mean(x * x, axis=-1, keepdims=True) + NORM_EPS) * g_ref[...]


def _dot_nt(a, b):
    return lax.dot_general(a, b, (((1,), (1,)), ((), ())), preferred_element_type=F32)


def _mla_proj_kernel(c_ref, d_ref, cos_ref, sin_ref, cost_ref, sint_ref, gq_ref, gkv_ref,
                     wqn_ref, wq1_ref, wq2_ref, wuk_ref, wuv_ref, qt_ref, k_ref, vt_ref):
    cq = c_ref[0, :, 0:MLA_RANK].astype(F32)
    ckv = c_ref[0, :, MLA_RANK:2 * MLA_RANK].astype(F32)
    qn = _rms(cq, gq_ref).astype(BF16)
    kvn = _rms(ckv, gkv_ref).astype(BF16)

    q_nope_t = _dot_nt(wqn_ref[...], qn)
    q1_t = _dot_nt(wq1_ref[...], qn)
    q2_t = _dot_nt(wq2_ref[...], qn)
    cos_t = jnp.concatenate([cost_ref[0]] * MLA_HEADS, axis=0)
    sin_t = jnp.concatenate([sint_ref[0]] * MLA_HEADS, axis=0)
    qr1_t = q1_t * cos_t - q2_t * sin_t
    qr2_t = q1_t * sin_t + q2_t * cos_t

    k_nope = jnp.dot(kvn, wuk_ref[...], preferred_element_type=F32)
    v_t = _dot_nt(wuv_ref[...], kvn)
    kr = d_ref[0]
    k1 = kr[:, D_KR:D_KR + MLA_HALF]
    k2 = kr[:, D_KR + MLA_HALF:D_KR + MLA_ROPE]
    cos = cos_ref[0]
    sin = sin_ref[0]
    kr1 = (k1 * cos - k2 * sin).astype(BF16)
    kr2 = (k1 * sin + k2 * cos).astype(BF16)
    for h in range(MLA_HEADS):
        n0, n1 = h * MLA_NOPE, (h + 1) * MLA_NOPE
        h0, h1 = h * MLA_HALF, (h + 1) * MLA_HALF
        q_t = jnp.concatenate([q_nope_t[n0:n1], qr1_t[h0:h1], qr2_t[h0:h1]], axis=0)
        qt_ref[0, h] = (q_t * QK_SCALE_LOG2).astype(BF16)
        k_ref[0, h] = jnp.concatenate([k_nope[:, n0:n1].astype(BF16), kr1, kr2], axis=-1)
        vt_ref[0, h] = v_t[h * MLA_DV:(h + 1) * MLA_DV].astype(BF16)


def _mla_proj(cg, dg, tables, gq, gkv, wqn_t, wq1_t, wq2_t, wuk, wuv_t, l):
    bsz, seq, _ = cg.shape
    tm = _tile(seq, PROJ_ROWS)
    cos, sin, cos_t, sin_t = tables

    def row(w):
        return pl.BlockSpec((1, tm, w), lambda bb, i: (bb, i, 0))

    def col(w):
        return pl.BlockSpec((1, w, tm), lambda bb, i: (bb, 0, i))

    return pl.pallas_call(
        _mla_proj_kernel,
        out_shape=(jax.ShapeDtypeStruct((bsz, MLA_HEADS, MLA_QK, seq), BF16),
                   jax.ShapeDtypeStruct((bsz, MLA_HEADS, seq, MLA_QK), BF16),
                   jax.ShapeDtypeStruct((bsz, MLA_HEADS, MLA_DV, seq), BF16)),
        grid=(bsz, seq // tm),
        in_specs=[row(GRP_C), row(GRP_D), row(MLA_HALF), row(MLA_HALF),
                  col(MLA_HALF), col(MLA_HALF),
                  _layer_spec(gq, l), _layer_spec(gkv, l),
                  _layer_spec(wqn_t, l), _layer_spec(wq1_t, l), _layer_spec(wq2_t, l),
                  _layer_spec(wuk, l), _layer_spec(wuv_t, l)],
        out_specs=(pl.BlockSpec((1, MLA_HEADS, MLA_QK, tm), lambda bb, i: (bb, 0, 0, i)),
                   pl.BlockSpec((1, MLA_HEADS, tm, MLA_QK), lambda bb, i: (bb, 0, i, 0)),
                   pl.BlockSpec((1, MLA_HEADS, MLA_DV, tm), lambda bb, i: (bb, 0, 0, i))),
        compiler_params=_cparams(("parallel", "parallel")),
        name="mla_proj",
    )(cg, dg, cos, sin, cos_t, sin_t, gq, gkv, wqn_t, wq1_t, wq2_t, wuk, wuv_t)


def _attn_kernel(qt_ref, k_ref, vt_ref, o_ref, s_ref, m_ref, l_ref, acc_ref, *, tq):
    seq = k_ref.shape[2]
    nq = seq // tq
    key = lax.broadcasted_iota(jnp.int32, (tq, tq), 0)
    qry = lax.broadcasted_iota(jnp.int32, (tq, tq), 1)
    tri = key <= qry
    tasks = [(i, j) for i in range(nq) for j in range(i + 1)]

    def scores(t):
        i, j = tasks[t]
        qt = qt_ref[0, 0, :, i * tq:(i + 1) * tq]
        k = k_ref[0, 0, j * tq:(j + 1) * tq, :]
        s_ref[t % 2] = jnp.dot(k, qt, preferred_element_type=F32)

    def update(t):
        i, j = tasks[t]
        slot = i % 2
        vt = vt_ref[0, 0, :, j * tq:(j + 1) * tq]
        st = s_ref[t % 2]
        if j == i:
            st = jnp.where(tri, st, -jnp.inf)
        if j == 0:
            m_new = jnp.max(st, axis=0, keepdims=True)
            pt = jnp.exp2(st - m_new)
            l_new = jnp.sum(pt, axis=0, keepdims=True)
            acc_new = jnp.dot(vt, pt.astype(BF16), preferred_element_type=F32)
        else:
            m = m_ref[slot]
            m_new = jnp.maximum(m, jnp.max(st, axis=0, keepdims=True))
            pt = jnp.exp2(st - m_new)
            corr = jnp.exp2(m - m_new)
            l_new = corr * l_ref[slot] + jnp.sum(pt, axis=0, keepdims=True)
            acc_new = corr * acc_ref[slot] + jnp.dot(vt, pt.astype(BF16),
                                                     preferred_element_type=F32)
        if j == i:
            o_ref[0, i * tq:(i + 1) * tq, :] = (acc_new / l_new).T.astype(o_ref.dtype)
        else:
            m_ref[slot] = m_new
            l_ref[slot] = l_new
            acc_ref[slot] = acc_new

    scores(0)
    for t in range(len(tasks)):
        if t + 1 < len(tasks):
            scores(t + 1)
        update(t)


def _attention(qt, k, vt):
    bsz, nh, seq, _ = k.shape
    tq = _tile(seq, ATTN_BLOCK)
    return pl.pallas_call(
        functools.partial(_attn_kernel, tq=tq),
        out_shape=jax.ShapeDtypeStruct((bsz, seq, nh * MLA_DV), BF16),
        grid=(bsz, nh),
        in_specs=[
            pl.BlockSpec((1, 1, MLA_QK, seq), lambda bb, h: (bb, h, 0, 0)),
            pl.BlockSpec((1, 1, seq, MLA_QK), lambda bb, h: (bb, h, 0, 0)),
            pl.BlockSpec((1, 1, MLA_DV, seq), lambda bb, h: (bb, h, 0, 0)),
        ],
        out_specs=pl.BlockSpec((1, seq, MLA_DV), lambda bb, h: (bb, 0, h)),
        scratch_shapes=[pltpu.VMEM((2, tq, tq), F32), pltpu.VMEM((2, 1, tq), F32),
                        pltpu.VMEM((2, 1, tq), F32), pltpu.VMEM((2, MLA_DV, tq), F32)],
        compiler_params=_cparams(("parallel", "parallel")),
        name="mla_attn",
    )(qt, k, vt)


def _outproj_kernel(x_ref, mod_ref, yml_ref, ygla_ref, ymla_ref, w_ref, g_ref, b_ref, o_ref,
                    *, alpha):
    gate = 1.0 + mod_ref[0, 2:3, :]
    half = x_ref.shape[1] // 2
    for r in range(2):
        rows = pl.ds(r * half, half)
        y = jnp.concatenate([yml_ref[0, rows, :], ygla_ref[0, rows, :], ymla_ref[0, rows, :]],
                            axis=-1)
        res = jnp.dot(y, w_ref[...], preferred_element_type=F32) * gate
        o_ref[0, rows, :] = _residual_layer_norm(x_ref[0, rows, :], res, alpha, g_ref, b_ref)


def _outproj(x, mod, y_ml, y_gla, y_mla, w_out, ln_g, ln_b, l, alpha):
    bsz, seq, d = x.shape
    tm = _tile(seq, PROJ_ROWS)

    def row(w):
        return pl.BlockSpec((1, tm, w), lambda bb, i: (bb, i, 0))

    return pl.pallas_call(
        functools.partial(_outproj_kernel, alpha=alpha),
        out_shape=jax.ShapeDtypeStruct(x.shape, F32),
        grid=(bsz, seq // tm),
        in_specs=[
            row(d),
            _mod_spec(mod, l, 1),
            row(y_ml.shape[-1]), row(y_gla.shape[-1]), row(y_mla.shape[-1]),
            pl.BlockSpec((None,) + w_out.shape[1:], lambda bb, i: (l, 0, 0),
                         pipeline_mode=pl.Buffered(1)),
            _ln_spec(ln_g, l, 1),
            _ln_spec(ln_b, l, 1),
        ],
        out_specs=row(d),
        compiler_params=_cparams(("parallel", "parallel")),
        name="outproj",
    )(x, mod, y_ml, y_gla, y_mla, w_out, ln_g, ln_b)


def _cast_kernel(w_ref, o_ref):
    o_ref[...] = w_ref[...].astype(o_ref.dtype)


def _gate_up_bf16(wi):
    depth, d, f2 = wi.shape
    f = f2 // 2
    tf = _tile(f, CAST_COLS)
    nf = f // tf
    return pl.pallas_call(
        _cast_kernel,
        out_shape=jax.ShapeDtypeStruct((depth, 2, d, f), BF16),
        grid=(depth, 2, nf),
        in_specs=[pl.BlockSpec((None, d, tf), lambda l, h, j: (l, 0, h * nf + j))],
        out_specs=pl.BlockSpec((None, None, d, tf), lambda l, h, j: (l, h, 0, j)),
        compiler_params=_cparams(("parallel", "parallel", "parallel")),
        name="gate_up_cast",
    )(wi)


def _regroup_w_in(w_in):
    sizes = (ML_W, ML_W, ML_W, ML_HEADS, ML_HEADS, ML_W,
             GLA_KW, GLA_KW, GLA_VW, GLA_RANK, GLA_VW,
             MLA_RANK, MLA_RANK, MLA_ROPE)
    offs = np.concatenate([[0], np.cumsum(sizes)])
    (ml_q, ml_k, ml_v, ml_i, ml_f, ml_o, gl_q, gl_k, gl_v, gl_lr, gl_r, c_q, c_kv, k_r) = (
        w_in[..., int(offs[n]):int(offs[n + 1])] for n in range(len(sizes)))
    used = MLA_ROPE + 2 * ML_HEADS + GLA_RANK
    pad = jnp.zeros(w_in.shape[:-1] + (GRP_D - used,), w_in.dtype)
    return jnp.concatenate([ml_q, ml_k, ml_v, ml_o, gl_q, gl_k, gl_v, gl_r, c_q, c_kv,
                            k_r, ml_i, ml_f, gl_lr, pad], axis=-1).astype(BF16)


def kernel(x, c, positions, w_ada, b_ada, ln_g, ln_b, ffn1_wi, ffn1_wo, ffn2_wi, ffn2_wo,
           w_in, ml_conv, ml_bi, ml_bf, gla_wg, gla_bg, mla_gq, mla_wuq, mla_gkv,
           mla_wuk, mla_wuv, w_out):
    depth = w_ada.shape[0]
    bsz, seq, d = x.shape
    alpha = (2.0 * depth) ** 0.25

    mod = _ada_mod(c, w_ada, b_ada).reshape(depth, bsz, 3, 3, d)
    tables = _rope_tables(positions)

    w_in_r = _regroup_w_in(w_in)
    ffn1_wi_b, ffn1_wo_b = _gate_up_bf16(ffn1_wi), ffn1_wo.astype(BF16)
    ffn2_wi_b, ffn2_wo_b = _gate_up_bf16(ffn2_wi), ffn2_wo.astype(BF16)
    w_out_b = w_out.astype(BF16)
    ln_g4 = ln_g.reshape(depth, 3, 1, d)
    ln_b4 = ln_b.reshape(depth, 3, 1, d)
    wuq = mla_wuq.reshape(depth, MLA_RANK, MLA_HEADS, MLA_QK)

    def feat_major(w):
        return jnp.swapaxes(w.reshape(depth, MLA_RANK, -1), 1, 2).astype(BF16)

    wqn_t = feat_major(wuq[..., :MLA_NOPE])
    wq1_t = feat_major(wuq[..., MLA_NOPE:MLA_NOPE + MLA_HALF])
    wq2_t = feat_major(wuq[..., MLA_NOPE + MLA_HALF:])
    wuk_b = mla_wuk.astype(BF16)
    wuv_t = feat_major(mla_wuv)
    gq3 = mla_gq.reshape(depth, 1, MLA_RANK)
    gkv3 = mla_gkv.reshape(depth, 1, MLA_RANK)
    gate_bias = jnp.zeros((depth, 1, GRP_D), F32)
    gate_bias = gate_bias.at[:, 0, D_MLI:D_MLI + ML_HEADS].set(ml_bi)
    gate_bias = gate_bias.at[:, 0, D_MLF:D_MLF + ML_HEADS].set(ml_bf)
    wg_pad = jnp.zeros((depth, GRP_D, GLA_KW), F32)
    wg_pad = wg_pad.at[:, D_GLR:D_GLR + GLA_RANK, :].set(gla_wg).astype(BF16)
    gla_bg3 = gla_bg.reshape(depth, 1, GLA_KW)

    for l in range(depth):
        x = _ffn(x, mod, ffn1_wi_b, ffn1_wo_b, ln_g4, ln_b4, l, 0, alpha)

        ag, bg, cg, dg = _inproj(x, mod, w_in_r, l)
        y_ml, y_gla = _recurrent(ag, bg, dg, ml_conv, gate_bias, wg_pad, gla_bg3, l)
        qt, k, vt = _mla_proj(cg, dg, tables, gq3, gkv3, wqn_t, wq1_t, wq2_t, wuk_b, wuv_t, l)
        y_mla = _attention(qt, k, vt)
        x = _outproj(x, mod, y_ml, y_gla, y_mla, w_out_b, ln_g4, ln_b4, l, alpha)

        x = _ffn(x, mod, ffn2_wi_b, ffn2_wo_b, ln_g4, ln_b4, l, 2, alpha)
    return x
```

```python
import functools
import math

import jax
import jax.numpy as jnp
import numpy as np
from jax import lax
from jax.experimental import pallas as pl
from jax.experimental.pallas import tpu as pltpu

F32 = jnp.float32
BF16 = jnp.bfloat16

ML_HEADS = 4
ML_DH = 128
ML_W = ML_HEADS * ML_DH
ML_CONV = 4
GLA_HEADS = 4
GLA_DK = 64
GLA_DV = 128
GLA_KW = GLA_HEADS * GLA_DK
GLA_VW = GLA_HEADS * GLA_DV
GLA_RANK = 16
GLA_TAU = 16.0
GLA_CHUNK = 64
MLA_HEADS = 8
MLA_NOPE = 128
MLA_ROPE = 64
MLA_HALF = MLA_ROPE // 2
MLA_DV = 128
MLA_QK = MLA_NOPE + MLA_ROPE
MLA_RANK = 512
ROPE_THETA = 10000.0
NORM_EPS = 1e-5
FFN_RES_WEIGHT = 0.5
QK_SCALE_LOG2 = math.log2(math.e) * MLA_QK ** -0.5

GRP_A = 4 * ML_W
GRP_B = 2 * GLA_KW + 2 * GLA_VW
GRP_C = 2 * MLA_RANK
GRP_D = 128
D_KR = 0
D_MLI = MLA_ROPE
D_MLF = D_MLI + ML_HEADS
D_GLR = D_MLF + ML_HEADS

VMEM_LIMIT_V7X = 58 * 1024 * 1024

FFN_ROWS = 1024
FFN_COLS = 512
REC_BLOCK = 256
ATTN_BLOCK = 1024
PROJ_ROWS = 512
WIDE_ROWS = 1024
ADA_COLS = 1024
ROPE_ROWS = 1024
CAST_COLS = 1408


def _cparams(sem):
    return pltpu.CompilerParams(dimension_semantics=sem, vmem_limit_bytes=VMEM_LIMIT_V7X)


def _tile(n, pref):
    t = min(n, pref)
    while n % t:
        t //= 2
    return t


def _layer_spec(arr, l):
    shape = (None,) + arr.shape[1:]
    zeros = (0,) * (arr.ndim - 1)
    return pl.BlockSpec(shape, lambda *_: (l,) + zeros)


def _mod_spec(mod, l, k):
    d = mod.shape[-1]
    return pl.BlockSpec((None, 1, None, 3, d), lambda bb, *_: (l, bb, k, 0, 0))


def _ln_spec(ln, l, k):
    return pl.BlockSpec((None, None, 1, ln.shape[-1]), lambda *_: (l, k, 0, 0))


def _sigmoid(x):
    return jax.nn.sigmoid(x)


def _log_sigmoid(x):
    return jnp.minimum(x, 0.0) - jnp.log1p(jnp.exp(-jnp.abs(x)))


def _modulate(x, mod_ref):
    return x * (1.0 + mod_ref[0, 1:2, :]) + mod_ref[0, 0:1, :]


def _residual_layer_norm(x, r, alpha, g_ref, b_ref):
    z = alpha * x + r
    mu = jnp.mean(z, axis=-1, keepdims=True)
    zc = z - mu
    var = jnp.mean(zc * zc, axis=-1, keepdims=True)
    return zc * lax.rsqrt(var + NORM_EPS) * g_ref[...] + b_ref[...]


def _ada_kernel(c_ref, w_ref, b_ref, o_ref):
    c = c_ref[...]
    ca = (c * _sigmoid(c)).astype(BF16)
    o_ref[0] = jnp.dot(ca, w_ref[0].astype(BF16), preferred_element_type=F32) + b_ref[0]


def _ada_mod(c, w_ada, b_ada):
    depth, d, n = w_ada.shape
    bsz = c.shape[0]
    tn = _tile(n, ADA_COLS)
    return pl.pallas_call(
        _ada_kernel,
        out_shape=jax.ShapeDtypeStruct((depth, bsz, n), F32),
        grid=(depth, n // tn),
        in_specs=[
            pl.BlockSpec((bsz, d), lambda l, j: (0, 0)),
            pl.BlockSpec((1, d, tn), lambda l, j: (l, 0, j)),
            pl.BlockSpec((1, 1, tn), lambda l, j: (l, 0, j)),
        ],
        out_specs=pl.BlockSpec((1, bsz, tn), lambda l, j: (l, 0, j)),
        compiler_params=_cparams(("parallel", "parallel")),
        name="ada_mod",
    )(c, w_ada, b_ada.reshape(depth, 1, n))


def _rope_kernel(pcol_ref, prow_ref, frow_ref, fcol_ref, cos_ref, sin_ref, cost_ref, sint_ref):
    ang = pcol_ref[0].astype(F32) * frow_ref[...]
    cos_ref[0] = jnp.cos(ang)
    sin_ref[0] = jnp.sin(ang)
    ang_t = fcol_ref[...] * prow_ref[0].astype(F32)
    cost_ref[0] = jnp.cos(ang_t)
    sint_ref[0] = jnp.sin(ang_t)


def _rope_tables(positions):
    bsz, seq = positions.shape
    ts = _tile(seq, ROPE_ROWS)
    inv_freq = (ROPE_THETA ** (-np.arange(MLA_HALF, dtype=np.float32) / MLA_HALF)).astype(np.float32)
    nat = jax.ShapeDtypeStruct((bsz, seq, MLA_HALF), F32)
    tra = jax.ShapeDtypeStruct((bsz, MLA_HALF, seq), F32)
    return pl.pallas_call(
        _rope_kernel,
        out_shape=(nat, nat, tra, tra),
        grid=(bsz, seq // ts),
        in_specs=[
            pl.BlockSpec((1, ts, 1), lambda b, i: (b, i, 0)),
            pl.BlockSpec((1, 1, ts), lambda b, i: (b, 0, i)),
            pl.BlockSpec((1, MLA_HALF), lambda b, i: (0, 0)),
            pl.BlockSpec((MLA_HALF, 1), lambda b, i: (0, 0)),
        ],
        out_specs=(pl.BlockSpec((1, ts, MLA_HALF), lambda b, i: (b, i, 0)),
                   pl.BlockSpec((1, ts, MLA_HALF), lambda b, i: (b, i, 0)),
                   pl.BlockSpec((1, MLA_HALF, ts), lambda b, i: (b, 0, i)),
                   pl.BlockSpec((1, MLA_HALF, ts), lambda b, i: (b, 0, i))),
        compiler_params=_cparams(("parallel", "parallel")),
        name="rope_tables",
    )(positions.reshape(bsz, seq, 1), positions.reshape(bsz, 1, seq),
      jnp.asarray(inv_freq[None, :]), jnp.asarray(inv_freq[:, None]))


def _ffn_kernel(x_ref, mod_ref, wi_ref, wo_ref, g_ref, b_ref, o_ref, *, alpha, nf):
    j = pl.program_id(2)
    half = x_ref.shape[1] // 2
    gate = FFN_RES_WEIGHT * (1.0 + mod_ref[0, 2:3, :])

    def accumulate(r, first):
        rows = pl.ds(r * half, half)
        u = _modulate(x_ref[0, rows, :], mod_ref).astype(BF16)
        g = jnp.dot(u, wi_ref[0], preferred_element_type=F32)
        up = jnp.dot(u, wi_ref[1], preferred_element_type=F32)
        a = (g * _sigmoid(g) * up).astype(BF16)
        y = jnp.dot(a, wo_ref[...], preferred_element_type=F32)
        if first:
            o_ref[0, rows, :] = y
        else:
            o_ref[0, rows, :] += y

    def finish(r):
        for c in range(2):
            rows = pl.ds(r * half + c * (half // 2), half // 2)
            o_ref[0, rows, :] = _residual_layer_norm(x_ref[0, rows, :], o_ref[0, rows, :] * gate,
                                                     alpha, g_ref, b_ref)

    def step(first, last):
        accumulate(0, first)
        accumulate(1, first)
        if last:
            finish(0)
            finish(1)

    if nf == 1:
        step(True, True)
    else:
        pl.when(j == 0)(lambda: step(True, False))
        if nf > 2:
            pl.when((j > 0) & (j < nf - 1))(lambda: step(False, False))
        pl.when(j == nf - 1)(lambda: step(False, True))


def _ffn(x, mod, wi, wo, ln_g, ln_b, l, k, alpha):
    bsz, seq, d = x.shape
    f = wo.shape[1]
    tm = _tile(seq, FFN_ROWS)
    tf = _tile(f, FFN_COLS)
    nf = f // tf
    return pl.pallas_call(
        functools.partial(_ffn_kernel, alpha=alpha, nf=nf),
        out_shape=jax.ShapeDtypeStruct(x.shape, F32),
        grid=(bsz, seq // tm, nf),
        in_specs=[
            pl.BlockSpec((1, tm, d), lambda bb, i, j: (bb, i, 0)),
            _mod_spec(mod, l, k),
            pl.BlockSpec((None, 2, d, tf), lambda bb, i, j: (l, 0, 0, j)),
            pl.BlockSpec((None, tf, d), lambda bb, i, j: (l, j, 0)),
            _ln_spec(ln_g, l, k),
            _ln_spec(ln_b, l, k),
        ],
        out_specs=pl.BlockSpec((1, tm, d), lambda bb, i, j: (bb, i, 0)),
        compiler_params=_cparams(("parallel", "parallel", "arbitrary")),
        name="ffn",
    )(x, mod, wi, wo, ln_g, ln_b)


def _inproj_kernel(x_ref, mod_ref, w_ref, a_ref, b_ref, c_ref, d_ref):
    u = _modulate(x_ref[0], mod_ref).astype(BF16)
    o0, o1, o2, o3 = 0, GRP_A, GRP_A + GRP_B, GRP_A + GRP_B + GRP_C
    a_ref[0] = jnp.dot(u, w_ref[:, o0:o1], preferred_element_type=F32).astype(BF16)
    b_ref[0] = jnp.dot(u, w_ref[:, o1:o2], preferred_element_type=F32).astype(BF16)
    c_ref[0] = jnp.dot(u, w_ref[:, o2:o3], preferred_element_type=F32).astype(BF16)
    d_ref[0] = jnp.dot(u, w_ref[:, o3:o3 + GRP_D], preferred_element_type=F32)


def _inproj(x, mod, w_r, l):
    bsz, seq, d = x.shape
    tm = _tile(seq, PROJ_ROWS)
    n = w_r.shape[-1]

    def row(w):
        return pl.BlockSpec((1, tm, w), lambda bb, i: (bb, i, 0))

    return pl.pallas_call(
        _inproj_kernel,
        out_shape=(jax.ShapeDtypeStruct((bsz, seq, GRP_A), BF16),
                   jax.ShapeDtypeStruct((bsz, seq, GRP_B), BF16),
                   jax.ShapeDtypeStruct((bsz, seq, GRP_C), BF16),
                   jax.ShapeDtypeStruct((bsz, seq, GRP_D), F32)),
        grid=(bsz, seq // tm),
        in_specs=[
            row(d),
            _mod_spec(mod, l, 1),
            pl.BlockSpec((None, d, n), lambda bb, i: (l, 0, 0), pipeline_mode=pl.Buffered(1)),
        ],
        out_specs=(row(GRP_A), row(GRP_B), row(GRP_C), row(GRP_D)),
        compiler_params=_cparams(("parallel", "parallel")),
        name="inproj",
    )(x, mod, w_r)


ML_PAD = 8


def _mlstm_step(a_ref, d_ref, conv_ref, bias_ref, o_ref, xbuf, c_scr, n_scr, m_scr):
    blk = a_ref.shape[1]
    pad = ML_PAD

    xbuf[pad:pad + blk, :] = a_ref[0, :, 0:2 * ML_W].astype(F32)
    qk = conv_ref[ML_CONV - 1:ML_CONV, :] * xbuf[pad:pad + blk, :]
    for tap in range(ML_CONV - 1):
        sh = ML_CONV - 1 - tap
        qk = qk + conv_ref[tap:tap + 1, :] * xbuf[pad - sh:pad - sh + blk, :]
    qk = qk * _sigmoid(qk)
    xbuf[0:pad, :] = xbuf[blk:blk + pad, :]

    pre = d_ref[0] + bias_ref[...]
    row = lax.broadcasted_iota(jnp.int32, (blk, blk), 0)
    col = lax.broadcasted_iota(jnp.int32, (blk, blk), 1)
    causal = col <= row
    eye = col == row
    tri = jnp.where(causal, 1.0, 0.0).astype(F32)
    bcum = jnp.dot(tri, _log_sigmoid(pre), preferred_element_type=F32,
                   precision=lax.Precision.HIGHEST)

    outs = []
    for h in range(ML_HEADS):
        b_col = bcum[:, D_MLF + h:D_MLF + h + 1]
        r_col = pre[:, D_MLI + h:D_MLI + h + 1] - b_col
        r_row = jnp.sum(jnp.where(eye, r_col, 0.0), axis=0, keepdims=True)
        b_last = b_col[blk - 1:blk, :]
        m_prev = m_scr[h, 0:1, 0:1]
        n_prev = n_scr[h]
        c_prev = c_scr[h]

        q = qk[:, h * ML_DH:(h + 1) * ML_DH]
        k = qk[:, ML_W + h * ML_DH:ML_W + (h + 1) * ML_DH] * (ML_DH ** -0.5)
        v = a_ref[0, :, 2 * ML_W + h * ML_DH:2 * ML_W + (h + 1) * ML_DH]
        og = a_ref[0, :, 3 * ML_W + h * ML_DH:3 * ML_W + (h + 1) * ML_DH].astype(F32)
        qb = q.astype(BF16)

        dmat = jnp.where(causal, b_col + r_row, -jnp.inf)
        m_inter = b_col + m_prev
        m_t = jnp.maximum(m_inter, jnp.max(dmat, axis=1, keepdims=True))
        s = lax.dot_general(qb, k.astype(BF16), (((1,), (1,)), ((), ())),
                            preferred_element_type=F32) * jnp.exp(dmat - m_t)
        inter = jnp.exp(m_inter - m_t)
        num = (jnp.dot(s.astype(BF16), v, preferred_element_type=F32)
               + inter * jnp.dot(qb, c_prev.astype(BF16), preferred_element_type=F32))
        den = (jnp.sum(s, axis=1, keepdims=True)
               + inter * jnp.sum(q * n_prev, axis=1, keepdims=True))
        hh = num / jnp.maximum(jnp.abs(den), jnp.exp(-m_t))
        mu = jnp.mean(hh, axis=-1, keepdims=True)
        hc = hh - mu
        var = jnp.mean(hc * hc, axis=-1, keepdims=True)
        outs.append(_sigmoid(og) * (hc * lax.rsqrt(var + NORM_EPS)))

        a_col = b_last + r_col
        a_max = jnp.max(a_col, axis=0, keepdims=True)
        kw = k * jnp.exp(a_col - a_max)
        c_loc = lax.dot_general(kw.astype(BF16), v, (((0,), (0,)), ((), ())),
                                preferred_element_type=F32)
        n_loc = jnp.sum(kw, axis=0, keepdims=True)
        m_new = jnp.maximum(b_last + m_prev, a_max)
        dec = jnp.exp(b_last + m_prev - m_new)
        inj = jnp.exp(a_max - m_new)
        c_scr[h] = dec * c_prev + inj * c_loc
        n_scr[h] = dec * n_prev + inj * n_loc
        m_scr[h] = jnp.broadcast_to(m_new, m_scr.shape[1:])

    o_ref[0] = jnp.concatenate(outs, axis=-1).astype(o_ref.dtype)


def _gla_step(b_ref, d_ref, wg_ref, bg_ref, o_ref, st_scr):
    blk = b_ref.shape[1]
    ch = GLA_CHUNK

    gate_pre = jnp.dot(d_ref[0].astype(BF16), wg_ref[...],
                       preferred_element_type=F32) + bg_ref[...]
    log_a = _log_sigmoid(gate_pre) / GLA_TAU
    row = lax.broadcasted_iota(jnp.int32, (blk, blk), 0)
    col = lax.broadcasted_iota(jnp.int32, (blk, blk), 1)
    same_chunk = (row // ch) == (col // ch)
    tri = jnp.where((col <= row) & same_chunk, 1.0, 0.0).astype(F32)
    bc = jnp.dot(tri, log_a, preferred_element_type=F32,
                 precision=lax.Precision.HIGHEST)
    e_pos = jnp.exp(bc)
    e_neg = jnp.exp(-bc)
    crow = lax.broadcasted_iota(jnp.int32, (ch, ch), 0)
    ccol = lax.broadcasted_iota(jnp.int32, (ch, ch), 1)
    causal = ccol <= crow

    for sc in range(blk // ch):
        r0, r1 = sc * ch, (sc + 1) * ch
        outs = []
        for h in range(GLA_HEADS):
            k0, k1 = h * GLA_DK, (h + 1) * GLA_DK
            q = b_ref[0, r0:r1, k0:k1].astype(F32) * (GLA_DK ** -0.5)
            k = b_ref[0, r0:r1, GLA_KW + k0:GLA_KW + k1].astype(F32)
            v = b_ref[0, r0:r1, 2 * GLA_KW + h * GLA_DV:2 * GLA_KW + (h + 1) * GLA_DV]
            rg = b_ref[0, r0:r1, 2 * GLA_KW + GLA_VW + h * GLA_DV:
                       2 * GLA_KW + GLA_VW + (h + 1) * GLA_DV].astype(F32)
            bc_h = bc[r0:r1, k0:k1]
            b_last = bc_h[ch - 1:ch, :]
            q_dec = (q * e_pos[r0:r1, k0:k1]).astype(BF16)
            k_inv = (k * e_neg[r0:r1, k0:k1]).astype(BF16)
            attn = lax.dot_general(q_dec, k_inv, (((1,), (1,)), ((), ())),
                                   preferred_element_type=F32)
            attn = jnp.where(causal, attn, 0.0)
            st = st_scr[h]
            o = (jnp.dot(attn.astype(BF16), v, preferred_element_type=F32)
                 + lax.dot_general(q_dec, st.astype(BF16), (((1,), (1,)), ((), ())),
                                   preferred_element_type=F32))
            k_w = (k * jnp.exp(b_last - bc_h)).astype(BF16)
            s_loc_t = lax.dot_general(v, k_w, (((0,), (0,)), ((), ())),
                                      preferred_element_type=F32)
            st_scr[h] = st * jnp.exp(b_last) + s_loc_t
            on = o * lax.rsqrt(jnp.mean(o * o, axis=-1, keepdims=True) + NORM_EPS)
            outs.append(rg * _sigmoid(rg) * on)
        o_ref[0, r0:r1, :] = jnp.concatenate(outs, axis=-1).astype(o_ref.dtype)


def _recurrent_kernel(a_ref, b_ref, d_ref, conv_ref, bias_ref, wg_ref, bg_ref, oml_ref, ogla_ref,
                      xbuf, c_scr, n_scr, m_scr, st_scr):
    @pl.when(pl.program_id(1) == 0)
    def _():
        xbuf[0:ML_PAD, :] = jnp.zeros((ML_PAD, 2 * ML_W), F32)
        c_scr[...] = jnp.zeros_like(c_scr)
        n_scr[...] = jnp.zeros_like(n_scr)
        m_scr[...] = jnp.zeros_like(m_scr)
        st_scr[...] = jnp.zeros_like(st_scr)

    _mlstm_step(a_ref, d_ref, conv_ref, bias_ref, oml_ref, xbuf, c_scr, n_scr, m_scr)
    _gla_step(b_ref, d_ref, wg_ref, bg_ref, ogla_ref, st_scr)


def _recurrent(ag, bg, dg, conv, gate_bias, wg_pad, gla_bg, l):
    bsz, seq, _ = ag.shape
    blk = _tile(seq, REC_BLOCK)

    def row(w):
        return pl.BlockSpec((1, blk, w), lambda bb, c: (bb, c, 0))

    return pl.pallas_call(
        _recurrent_kernel,
        out_shape=(jax.ShapeDtypeStruct((bsz, seq, ML_W), BF16),
                   jax.ShapeDtypeStruct((bsz, seq, GLA_VW), BF16)),
        grid=(bsz, seq // blk),
        in_specs=[row(GRP_A), row(GRP_B), row(GRP_D),
                  _layer_spec(conv, l), _layer_spec(gate_bias, l),
                  _layer_spec(wg_pad, l), _layer_spec(gla_bg, l)],
        out_specs=(row(ML_W), row(GLA_VW)),
        scratch_shapes=[
            pltpu.VMEM((blk + ML_PAD, 2 * ML_W), F32),
            pltpu.VMEM((ML_HEADS, ML_DH, ML_DH), F32),
            pltpu.VMEM((ML_HEADS, 1, ML_DH), F32),
            pltpu.VMEM((ML_HEADS, 1, ML_DH), F32),
            pltpu.VMEM((GLA_HEADS, GLA_DV, GLA_DK), F32),
        ],
        compiler_params=_cparams(("parallel", "arbitrary")),
        name="recurrent",
    )(ag, bg, dg, conv, gate_bias, wg_pad, gla_bg)


def _rms(x, g_ref):
    return x * lax.rsqrt(jnp.mean(x * x, axis=-1, keepdims=True) + NORM_EPS) * g_ref[...]


def _dot_nt(a, b):
    return lax.dot_general(a, b, (((1,), (1,)), ((), ())), preferred_element_type=F32)


def _mla_proj_kernel(c_ref, d_ref, cos_ref, sin_ref, cost_ref, sint_ref, gq_ref, gkv_ref,
                     wqn_ref, wq1_ref, wq2_ref, wuk_ref, wuv_ref, qt_ref, k_ref, vt_ref):
    cq = c_ref[0, :, 0:MLA_RANK].astype(F32)
    ckv = c_ref[0, :, MLA_RANK:2 * MLA_RANK].astype(F32)
    qn = _rms(cq, gq_ref).astype(BF16)
    kvn = _rms(ckv, gkv_ref).astype(BF16)

    q_nope_t = _dot_nt(wqn_ref[...], qn)
    q1_t = _dot_nt(wq1_ref[...], qn)
    q2_t = _dot_nt(wq2_ref[...], qn)
    cos_t = jnp.concatenate([cost_ref[0]] * MLA_HEADS, axis=0)
    sin_t = jnp.concatenate([sint_ref[0]] * MLA_HEADS, axis=0)
    qr1_t = q1_t * cos_t - q2_t * sin_t
    qr2_t = q1_t * sin_t + q2_t * cos_t

    k_nope = jnp.dot(kvn, wuk_ref[...], preferred_element_type=F32)
    v_t = _dot_nt(wuv_ref[...], kvn)
    kr = d_ref[0]
    k1 = kr[:, D_KR:D_KR + MLA_HALF]
    k2 = kr[:, D_KR + MLA_HALF:D_KR + MLA_ROPE]
    cos = cos_ref[0]
    sin = sin_ref[0]
    kr1 = (k1 * cos - k2 * sin).astype(BF16)
    kr2 = (k1 * sin + k2 * cos).astype(BF16)
    for h in range(MLA_HEADS):
        n0, n1 = h * MLA_NOPE, (h + 1) * MLA_NOPE
        h0, h1 = h * MLA_HALF, (h + 1) * MLA_HALF
        q_t = jnp.concatenate([q_nope_t[n0:n1], qr1_t[h0:h1], qr2_t[h0:h1]], axis=0)
        qt_ref[0, h] = (q_t * QK_SCALE_LOG2).astype(BF16)
        k_ref[0, h] = jnp.concatenate([k_nope[:, n0:n1].astype(BF16), kr1, kr2], axis=-1)
        vt_ref[0, h] = v_t[h * MLA_DV:(h + 1) * MLA_DV].astype(BF16)


def _mla_proj(cg, dg, tables, gq, gkv, wqn_t, wq1_t, wq2_t, wuk, wuv_t, l):
    bsz, seq, _ = cg.shape
    tm = _tile(seq, WIDE_ROWS)
    cos, sin, cos_t, sin_t = tables

    def row(w):
        return pl.BlockSpec((1, tm, w), lambda bb, i: (bb, i, 0))

    def col(w):
        return pl.BlockSpec((1, w, tm), lambda bb, i: (bb, 0, i))

    return pl.pallas_call(
        _mla_proj_kernel,
        out_shape=(jax.ShapeDtypeStruct((bsz, MLA_HEADS, MLA_QK, seq), BF16),
                   jax.ShapeDtypeStruct((bsz, MLA_HEADS, seq, MLA_QK), BF16),
                   jax.ShapeDtypeStruct((bsz, MLA_HEADS, MLA_DV, seq), BF16)),
        grid=(bsz, seq // tm),
        in_specs=[row(GRP_C), row(GRP_D), row(MLA_HALF), row(MLA_HALF),
                  col(MLA_HALF), col(MLA_HALF),
                  _layer_spec(gq, l), _layer_spec(gkv, l),
                  _layer_spec(wqn_t, l), _layer_spec(wq1_t, l), _layer_spec(wq2_t, l),
                  _layer_spec(wuk, l), _layer_spec(wuv_t, l)],
        out_specs=(pl.BlockSpec((1, MLA_HEADS, MLA_QK, tm), lambda bb, i: (bb, 0, 0, i)),
                   pl.BlockSpec((1, MLA_HEADS, tm, MLA_QK), lambda bb, i: (bb, 0, i, 0)),
                   pl.BlockSpec((1, MLA_HEADS, MLA_DV, tm), lambda bb, i: (bb, 0, 0, i))),
        compiler_params=_cparams(("parallel", "parallel")),
        name="mla_proj",
    )(cg, dg, cos, sin, cos_t, sin_t, gq, gkv, wqn_t, wq1_t, wq2_t, wuk, wuv_t)


def _attn_kernel(qt_ref, k_ref, vt_ref, o_ref, s_ref, m_ref, l_ref, acc_ref, *, tq):
    seq = k_ref.shape[2]
    nq = seq // tq
    key = lax.broadcasted_iota(jnp.int32, (tq, tq), 0)
    qry = lax.broadcasted_iota(jnp.int32, (tq, tq), 1)
    tri = key <= qry
    tasks = [(i, j) for i in range(nq) for j in range(i + 1)]

    def scores(t):
        i, j = tasks[t]
        qt = qt_ref[0, 0, :, i * tq:(i + 1) * tq]
        k = k_ref[0, 0, j * tq:(j + 1) * tq, :]
        s_ref[t % 2] = jnp.dot(k, qt, preferred_element_type=F32)

    def update(t):
        i, j = tasks[t]
        slot = i % 2
        vt = vt_ref[0, 0, :, j * tq:(j + 1) * tq]
        st = s_ref[t % 2]
        if j == i:
            st = jnp.where(tri, st, -jnp.inf)
        if j == 0:
            m_new = jnp.max(st, axis=0, keepdims=True)
            pt = jnp.exp2(st - m_new)
            l_new = jnp.sum(pt, axis=0, keepdims=True)
            acc_new = jnp.dot(vt, pt.astype(BF16), preferred_element_type=F32)
        else:
            m = m_ref[slot]
            m_new = jnp.maximum(m, jnp.max(st, axis=0, keepdims=True))
            pt = jnp.exp2(st - m_new)
            corr = jnp.exp2(m - m_new)
            l_new = corr * l_ref[slot] + jnp.sum(pt, axis=0, keepdims=True)
            acc_new = corr * acc_ref[slot] + jnp.dot(vt, pt.astype(BF16),
                                                     preferred_element_type=F32)
        if j == i:
            o_ref[0, i * tq:(i + 1) * tq, :] = (acc_new / l_new).T.astype(o_ref.dtype)
        else:
            m_ref[slot] = m_new
            l_ref[slot] = l_new
            acc_ref[slot] = acc_new

    scores(0)
    for t in range(len(tasks)):
        if t + 1 < len(tasks):
            scores(t + 1)
        update(t)


def _attention(qt, k, vt):
    bsz, nh, seq, _ = k.shape
    tq = _tile(seq, ATTN_BLOCK)
    return pl.pallas_call(
        functools.partial(_attn_kernel, tq=tq),
        out_shape=jax.ShapeDtypeStruct((bsz, seq, nh * MLA_DV), BF16),
        grid=(bsz, nh),
        in_specs=[
            pl.BlockSpec((1, 1, MLA_QK, seq), lambda bb, h: (bb, h, 0, 0)),
            pl.BlockSpec((1, 1, seq, MLA_QK), lambda bb, h: (bb, h, 0, 0)),
            pl.BlockSpec((1, 1, MLA_DV, seq), lambda bb, h: (bb, h, 0, 0)),
        ],
        out_specs=pl.BlockSpec((1, seq, MLA_DV), lambda bb, h: (bb, 0, h)),
        scratch_shapes=[pltpu.VMEM((2, tq, tq), F32), pltpu.VMEM((2, 1, tq), F32),
                        pltpu.VMEM((2, 1, tq), F32), pltpu.VMEM((2, MLA_DV, tq), F32)],
        compiler_params=_cparams(("parallel", "parallel")),
        name="mla_attn",
    )(qt, k, vt)


def _outproj_kernel(x_ref, mod_ref, yml_ref, ygla_ref, ymla_ref, w_ref, g_ref, b_ref, o_ref,
                    *, alpha):
    gate = 1.0 + mod_ref[0, 2:3, :]
    half = x_ref.shape[1] // 2
    for r in range(2):
        rows = pl.ds(r * half, half)
        y = jnp.concatenate([yml_ref[0, rows, :], ygla_ref[0, rows, :], ymla_ref[0, rows, :]],
                            axis=-1)
        res = jnp.dot(y, w_ref[...], preferred_element_type=F32) * gate
        o_ref[0, rows, :] = _residual_layer_norm(x_ref[0, rows, :], res, alpha, g_ref, b_ref)


def _outproj(x, mod, y_ml, y_gla, y_mla, w_out, ln_g, ln_b, l, alpha):
    bsz, seq, d = x.shape
    tm = _tile(seq, WIDE_ROWS)

    def row(w):
        return pl.BlockSpec((1, tm, w), lambda bb, i: (bb, i, 0))

    return pl.pallas_call(
        functools.partial(_outproj_kernel, alpha=alpha),
        out_shape=jax.ShapeDtypeStruct(x.shape, F32),
        grid=(bsz, seq // tm),
        in_specs=[
            row(d),
            _mod_spec(mod, l, 1),
            row(y_ml.shape[-1]), row(y_gla.shape[-1]), row(y_mla.shape[-1]),
            pl.BlockSpec((None,) + w_out.shape[1:], lambda bb, i: (l, 0, 0),
                         pipeline_mode=pl.Buffered(1)),
            _ln_spec(ln_g, l, 1),
            _ln_spec(ln_b, l, 1),
        ],
        out_specs=row(d),
        compiler_params=_cparams(("parallel", "parallel")),
        name="outproj",
    )(x, mod, y_ml, y_gla, y_mla, w_out, ln_g, ln_b)


def _cast_kernel(w_ref, o_ref):
    o_ref[...] = w_ref[...].astype(o_ref.dtype)


def _gate_up_bf16(wi):
    depth, d, f2 = wi.shape
    f = f2 // 2
    tf = _tile(f, CAST_COLS)
    nf = f // tf
    return pl.pallas_call(
        _cast_kernel,
        out_shape=jax.ShapeDtypeStruct((depth, 2, d, f), BF16),
        grid=(depth, 2, nf),
        in_specs=[pl.BlockSpec((None, d, tf), lambda l, h, j: (l, 0, h * nf + j))],
        out_specs=pl.BlockSpec((None, None, d, tf), lambda l, h, j: (l, h, 0, j)),
        compiler_params=_cparams(("parallel", "parallel", "parallel")),
        name="gate_up_cast",
    )(wi)


def _regroup_w_in(w_in):
    sizes = (ML_W, ML_W, ML_W, ML_HEADS, ML_HEADS, ML_W,
             GLA_KW, GLA_KW, GLA_VW, GLA_RANK, GLA_VW,
             MLA_RANK, MLA_RANK, MLA_ROPE)
    offs = np.concatenate([[0], np.cumsum(sizes)])
    (ml_q, ml_k, ml_v, ml_i, ml_f, ml_o, gl_q, gl_k, gl_v, gl_lr, gl_r, c_q, c_kv, k_r) = (
        w_in[..., int(offs[n]):int(offs[n + 1])] for n in range(len(sizes)))
    used = MLA_ROPE + 2 * ML_HEADS + GLA_RANK
    pad = jnp.zeros(w_in.shape[:-1] + (GRP_D - used,), w_in.dtype)
    return jnp.concatenate([ml_q, ml_k, ml_v, ml_o, gl_q, gl_k, gl_v, gl_r, c_q, c_kv,
                            k_r, ml_i, ml_f, gl_lr, pad], axis=-1).astype(BF16)


def kernel(x, c, positions, w_ada, b_ada, ln_g, ln_b, ffn1_wi, ffn1_wo, ffn2_wi, ffn2_wo,
           w_in, ml_conv, ml_bi, ml_bf, gla_wg, gla_bg, mla_gq, mla_wuq, mla_gkv,
           mla_wuk, mla_wuv, w_out):
    depth = w_ada.shape[0]
    bsz, seq, d = x.shape
    alpha = (2.0 * depth) ** 0.25

    mod = _ada_mod(c, w_ada, b_ada).reshape(depth, bsz, 3, 3, d)
    tables = _rope_tables(positions)

    w_in_r = _regroup_w_in(w_in)
    ffn1_wi_b, ffn1_wo_b = _gate_up_bf16(ffn1_wi), ffn1_wo.astype(BF16)
    ffn2_wi_b, ffn2_wo_b = _gate_up_bf16(ffn2_wi), ffn2_wo.astype(BF16)
    w_out_b = w_out.astype(BF16)
    ln_g4 = ln_g.reshape(depth, 3, 1, d)
    ln_b4 = ln_b.reshape(depth, 3, 1, d)
    wuq = mla_wuq.reshape(depth, MLA_RANK, MLA_HEADS, MLA_QK)

    def feat_major(w):
        return jnp.swapaxes(w.reshape(depth, MLA_RANK, -1), 1, 2).astype(BF16)

    wqn_t = feat_major(wuq[..., :MLA_NOPE])
    wq1_t = feat_major(wuq[..., MLA_NOPE:MLA_NOPE + MLA_HALF])
    wq2_t = feat_major(wuq[..., MLA_NOPE + MLA_HALF:])
    wuk_b = mla_wuk.astype(BF16)
    wuv_t = feat_major(mla_wuv)
    gq3 = mla_gq.reshape(depth, 1, MLA_RANK)
    gkv3 = mla_gkv.reshape(depth, 1, MLA_RANK)
    gate_bias = jnp.zeros((depth, 1, GRP_D), F32)
    gate_bias = gate_bias.at[:, 0, D_MLI:D_MLI + ML_HEADS].set(ml_bi)
    gate_bias = gate_bias.at[:, 0, D_MLF:D_MLF + ML_HEADS].set(ml_bf)
    wg_pad = jnp.zeros((depth, GRP_D, GLA_KW), F32)
    wg_pad = wg_pad.at[:, D_GLR:D_GLR + GLA_RANK, :].set(gla_wg).astype(BF16)
    gla_bg3 = gla_bg.reshape(depth, 1, GLA_KW)

    for l in range(depth):
        x = _ffn(x, mod, ffn1_wi_b, ffn1_wo_b, ln_g4, ln_b4, l, 0, alpha)

        ag, bg, cg, dg = _inproj(x, mod, w_in_r, l)
        y_ml, y_gla = _recurrent(ag, bg, dg, ml_conv, gate_bias, wg_pad, gla_bg3, l)
        qt, k, vt = _mla_proj(cg, dg, tables, gq3, gkv3, wqn_t, wq1_t, wq2_t, wuk_b, wuv_t, l)
        y_mla = _attention(qt, k, vt)
        x = _outproj(x, mod, y_ml, y_gla, y_mla, w_out_b, ln_g4, ln_b4, l, alpha)

        x = _ffn(x, mod, ffn2_wi_b, ffn2_wo_b, ln_g4, ln_b4, l, 2, alpha)
    return x
```
